```python
import math
import jax
import jax.numpy as jnp
from jax import lax
import numpy as np

D_MODEL = 4096
BATCH = 8
SEQ = 2048
DEPTH = 2

GRID_W = 64
CTX_LEN = 256
N_EVEN = (DEPTH + 1) // 2
N_ODD = DEPTH // 2
EPS = 1e-6
ROPE_THETA = 10000.0
Q_BLOCK = 128

POOL_WINDOWS = (2, 4, 8, 16)
POOL_WIDTH = D_MODEL // 2
POOL_GROUP = POOL_WIDTH // len(POOL_WINDOWS)
SGU_WIDTH = D_MODEL // 2
SGU_HEADS = 4
SGU_HEAD_DIM = SGU_WIDTH // SGU_HEADS
SGU_CHUNK = 128
EVEN_IN = POOL_WIDTH + 2 * SGU_WIDTH
EVEN_OUT = POOL_WIDTH + SGU_WIDTH
MLA_HEADS = 16
MLA_Q_RANK = 1024
MLA_KV_RANK = 512
MLA_NOPE = 128
MLA_ROPE = 64
MLA_QK = MLA_NOPE + MLA_ROPE
MLA_V = 128
MLA_SCALE = 1.0 / math.sqrt(MLA_QK)
GQA_HEADS = 16
GQA_KV_HEADS = 4
GQA_HEAD_DIM = 128
GQA_SCALE = 1.0 / math.sqrt(GQA_HEAD_DIM)
ODD_Q_COLS = MLA_Q_RANK + GQA_HEADS * GQA_HEAD_DIM
KV_OFF_KR = MLA_KV_RANK
KV_OFF_GK = KV_OFF_KR + MLA_ROPE
KV_OFF_GV = KV_OFF_GK + GQA_KV_HEADS * GQA_HEAD_DIM
ODD_KV_COLS = KV_OFF_GV + GQA_KV_HEADS * GQA_HEAD_DIM
ODD_IN = ODD_Q_COLS + ODD_KV_COLS
ODD_OUT = MLA_HEADS * MLA_V + GQA_HEADS * GQA_HEAD_DIM
FFN_HIDDEN = ((8 * D_MODEL + 3 * 256 - 1) // (3 * 256)) * 256

kernel_name = "hybrid_pool_sgu_mla_gqa_dit_trunk"


def rms_norm(x, g):
    xf = x.astype(jnp.float32)
    y = xf * lax.rsqrt(jnp.mean(xf * xf, axis=-1, keepdims=True) + EPS)
    return (y * g.astype(jnp.float32)).astype(x.dtype)


def adaln(cvec, w, b):
    m = jax.nn.silu(cvec) @ w + b
    return jnp.split(m[:, None, :], 6, axis=-1)


def modulate(h, shift, scale):
    return h * (1 + scale) + shift


def axial_rope_table(rows, rot_dim):
    row = jnp.repeat(jnp.arange(rows, dtype=jnp.float32), GRID_W)
    col = jnp.tile(jnp.arange(GRID_W, dtype=jnp.float32), rows)
    n_freq = rot_dim // 4
    inv = ROPE_THETA ** (-jnp.arange(n_freq, dtype=jnp.float32) / n_freq)
    ang = jnp.concatenate([row[:, None] * inv, col[:, None] * inv], axis=-1)
    return jnp.cos(ang), jnp.sin(ang)


def apply_rope(x, cos, sin):
    x1, x2 = jnp.split(x, 2, axis=-1)
    cs = cos[None, :, None, :]
    sn = sin[None, :, None, :]
    return jnp.concatenate([x1 * cs - x2 * sn, x1 * sn + x2 * cs], axis=-1).astype(x.dtype)


def centred_pool_minus_self(h, window):
    L = h.shape[1]
    half = window // 2
    cs = jnp.pad(jnp.cumsum(h.astype(jnp.float32), axis=1), ((0, 0), (1, 0), (0, 0)))
    t = jnp.arange(L)
    lo = jnp.clip(t - half, 0, L)
    hi = jnp.clip(t - half + window, 0, L)
    cnt = (hi - lo).astype(jnp.float32)
    mean = (cs[:, hi] - cs[:, lo]) / cnt[None, :, None]
    return mean.astype(h.dtype) - h


def pool_mixer(a, pool_w, pool_scale):
    groups = jnp.split(a, len(POOL_WINDOWS), axis=-1)
    pooled = jnp.stack([centred_pool_minus_self(g, w) for g, w in zip(groups, POOL_WINDOWS)], axis=2)
    mixed = jnp.einsum('blgc,gcd->blgd', pooled, pool_w)
    return mixed.reshape(a.shape) * pool_scale


def sgu_mixer(uv, norm_g, w_s, b_s):
    B, L, _ = uv.shape
    u, v = jnp.split(jax.nn.gelu(uv), 2, axis=-1)
    v = v.reshape(B, L // SGU_CHUNK, SGU_CHUNK, SGU_HEADS, SGU_HEAD_DIM)
    v = rms_norm(v, norm_g)
    gate = jnp.einsum('hpq,bnqhd->bnphd', w_s, v) + b_s.T[None, None, :, :, None]
    return u * gate.reshape(B, L, SGU_WIDTH)


def even_mixer(h, w_in, pool_w, pool_scale, sgu_norm_g, sgu_w_s, sgu_b, w_out):
    z = h @ w_in
    ya = pool_mixer(z[..., :POOL_WIDTH], pool_w, pool_scale)
    yb = sgu_mixer(z[..., POOL_WIDTH:], sgu_norm_g, sgu_w_s, sgu_b)
    return jnp.concatenate([ya, yb], axis=-1) @ w_out


def odd_queries(zq, q_a_g, w_q_b, q_norm_g, rope):
    B, L, _ = zq.shape
    q = (rms_norm(zq[..., :MLA_Q_RANK], q_a_g) @ w_q_b).reshape(B, L, MLA_HEADS, MLA_QK)
    gq = rms_norm(zq[..., MLA_Q_RANK:].reshape(B, L, GQA_HEADS, GQA_HEAD_DIM), q_norm_g)
    if rope is not None:
        cos_m, sin_m, cos_g, sin_g = rope
        q = jnp.concatenate([q[..., :MLA_NOPE], apply_rope(q[..., MLA_NOPE:], cos_m, sin_m)], axis=-1)
        gq = apply_rope(gq, cos_g, sin_g)
    return (q, gq)


def odd_keys(zkv, kv_a_g, w_kv_b, k_norm_g, rope):
    B, L, _ = zkv.shape
    kv = (rms_norm(zkv[..., :MLA_KV_RANK], kv_a_g) @ w_kv_b).reshape(B, L, MLA_HEADS, MLA_NOPE + MLA_V)
    k_nope, v = kv[..., :MLA_NOPE], kv[..., MLA_NOPE:]
    kr = zkv[..., KV_OFF_KR:KV_OFF_GK][:, :, None, :]
    gk = rms_norm(zkv[..., KV_OFF_GK:KV_OFF_GV].reshape(B, L, GQA_KV_HEADS, GQA_HEAD_DIM), k_norm_g)
    gv = zkv[..., KV_OFF_GV:].reshape(B, L, GQA_KV_HEADS, GQA_HEAD_DIM)
    if rope is not None:
        cos_m, sin_m, cos_g, sin_g = rope
        kr = apply_rope(kr, cos_m, sin_m)
        gk = apply_rope(gk, cos_g, sin_g)
    k = jnp.concatenate([k_nope, jnp.broadcast_to(kr, (B, L, MLA_HEADS, MLA_ROPE))], axis=-1)
    return (k, v, gk, gv)


def attend(q, k, v, scale):
    B, Lq, Hq, dk = q.shape
    Hkv = k.shape[2]
    G = Hq // Hkv
    dv = v.shape[-1]

    def block(qb):
        qb = qb.reshape(B, Q_BLOCK, Hkv, G, dk)
        s = jnp.einsum('bqhgd,bkhd->bhgqk', qb, k, preferred_element_type=jnp.float32) * scale
        p = jax.nn.softmax(s, axis=-1).astype(v.dtype)
        o = jnp.einsum('bhgqk,bkhd->bqhgd', p, v)
        return o.reshape(B, Q_BLOCK, Hq, dv)

    nb = Lq // Q_BLOCK
    qb = q.reshape(B, nb, Q_BLOCK, Hq, dk).transpose(1, 0, 2, 3, 4)
    o = lax.map(block, qb)
    return o.transpose(1, 0, 2, 3, 4).reshape(B, Lq, Hq, dv)


def odd_attend(qs, kvs, w_out):
    q_m, q_g = qs
    k_m, v_m, k_g, v_g = kvs
    B, L = q_m.shape[0], q_m.shape[1]
    o_m = attend(q_m, k_m, v_m, MLA_SCALE).reshape(B, L, MLA_HEADS * MLA_V)
    o_g = attend(q_g, k_g, v_g, GQA_SCALE).reshape(B, L, GQA_HEADS * GQA_HEAD_DIM)
    return jnp.concatenate([o_m, o_g], axis=-1) @ w_out


def swiglu(h, w_gu, w_down):
    g, u = jnp.split(h @ w_gu, 2, axis=-1)
    return (jax.nn.silu(g) * u) @ w_down


def setup_inputs(seed: int = 0) -> dict:
    key = jax.random.key(seed)
    ks = jax.random.split(key, 26)

    def nrm(i, shape, std):
        return jax.random.normal(ks[i], shape, jnp.float32) * std

    def gain(i, shape):
        return 1.0 + 0.05 * jax.random.normal(ks[i], shape, jnp.float32)

    D = D_MODEL
    return {
        "x": nrm(0, (BATCH, SEQ, D), 1.0),
        "c": nrm(1, (BATCH, D), 1.0),
        "ctx": nrm(2, (BATCH, CTX_LEN, D), 1.0),
        "c_ctx": nrm(3, (D,), 1.0),
        "mod_w": nrm(4, (DEPTH, D, 6 * D), 0.5 * D ** -0.5),
        "mod_b": nrm(5, (DEPTH, 6 * D), 0.02),
        "norm1_g": gain(6, (DEPTH, D)),
        "norm2_g": gain(7, (DEPTH, D)),
        "final_g": gain(8, (D,)),
        "ffn_w_gu": nrm(9, (DEPTH, D, 2 * FFN_HIDDEN), D ** -0.5),
        "ffn_w_down": nrm(10, (DEPTH, FFN_HIDDEN, D), FFN_HIDDEN ** -0.5),
        "ev_w_in": nrm(11, (N_EVEN, D, EVEN_IN), D ** -0.5),
        "ev_pool_w": nrm(12, (N_EVEN, len(POOL_WINDOWS), POOL_GROUP, POOL_GROUP), POOL_GROUP ** -0.5),
        "ev_pool_scale": gain(13, (N_EVEN, POOL_WIDTH)),
        "ev_sgu_norm_g": gain(14, (N_EVEN, SGU_HEADS, SGU_HEAD_DIM)),
        "ev_sgu_w_s": nrm(15, (N_EVEN, SGU_HEADS, SGU_CHUNK, SGU_CHUNK), SGU_CHUNK ** -0.5),
        "ev_sgu_b": gain(16, (N_EVEN, SGU_HEADS, SGU_CHUNK)),
        "ev_w_out": nrm(17, (N_EVEN, EVEN_OUT, D), EVEN_OUT ** -0.5),
        "od_w_in": nrm(18, (N_ODD, D, ODD_IN), D ** -0.5),
        "od_q_a_g": gain(19, (N_ODD, MLA_Q_RANK)),
        "od_w_q_b": nrm(20, (N_ODD, MLA_Q_RANK, MLA_HEADS * MLA_QK), MLA_Q_RANK ** -0.5),
        "od_kv_a_g": gain(21, (N_ODD, MLA_KV_RANK)),
        "od_w_kv_b": nrm(22, (N_ODD, MLA_KV_RANK, MLA_HEADS * (MLA_NOPE + MLA_V)), MLA_KV_RANK ** -0.5),
        "od_q_norm_g": gain(23, (N_ODD, GQA_HEAD_DIM)),
        "od_k_norm_g": gain(24, (N_ODD, GQA_HEAD_DIM)),
        "od_w_out": nrm(25, (N_ODD, ODD_OUT, D), ODD_OUT ** -0.5),
    }


def reference(x, c, ctx, c_ctx, mod_w, mod_b, norm1_g, norm2_g, final_g, ffn_w_gu, ffn_w_down,
              ev_w_in, ev_pool_w, ev_pool_scale, ev_sgu_norm_g, ev_sgu_w_s, ev_sgu_b, ev_w_out,
              od_w_in, od_q_a_g, od_w_q_b, od_kv_a_g, od_w_kv_b, od_q_norm_g, od_k_norm_g, od_w_out):
    rows = x.shape[1] // GRID_W
    rope = (*axial_rope_table(rows, MLA_ROPE), *axial_rope_table(rows, GQA_HEAD_DIM))
    c_ctx_row = c_ctx[None, :]
    for i in range(DEPTH):
        last = i == DEPTH - 1
        odd = i % 2 == 1
        j = i // 2
        ctx_live = (not last) or odd
        sh1, sc1, g1, sh2, sc2, g2 = adaln(c, mod_w[i], mod_b[i])
        hx = modulate(rms_norm(x, norm1_g[i]), sh1, sc1)
        if ctx_live:
            csh1, csc1, cg1, csh2, csc2, cg2 = adaln(c_ctx_row, mod_w[i], mod_b[i])
            hc = modulate(rms_norm(ctx, norm1_g[i]), csh1, csc1)
        if not odd:
            ev = (ev_w_in[j], ev_pool_w[j], ev_pool_scale[j], ev_sgu_norm_g[j], ev_sgu_w_s[j], ev_sgu_b[j],
                  ev_w_out[j])
            x = x + g1 * even_mixer(hx, *ev)
            if not last:
                ctx = ctx + cg1 * even_mixer(hc, *ev)
        else:
            w_in = od_w_in[j]
            zx = hx @ w_in
            kv_c = odd_keys(hc @ w_in[:, ODD_Q_COLS:], od_kv_a_g[j], od_w_kv_b[j], od_k_norm_g[j], None)
            kv_x = odd_keys(zx[..., ODD_Q_COLS:], od_kv_a_g[j], od_w_kv_b[j], od_k_norm_g[j], rope)
            kv_all = [jnp.concatenate([a, b], axis=1) for a, b in zip(kv_c, kv_x)]
            q_x = odd_queries(zx[..., :ODD_Q_COLS], od_q_a_g[j], od_w_q_b[j], od_q_norm_g[j], rope)
            x = x + g1 * odd_attend(q_x, kv_all, od_w_out[j])
            if not last:
                q_c = odd_queries(hc @ w_in[:, :ODD_Q_COLS], od_q_a_g[j], od_w_q_b[j], od_q_norm_g[j], None)
                ctx = ctx + cg1 * odd_attend(q_c, kv_c, od_w_out[j])
        x = x + g2 * swiglu(modulate(rms_norm(x, norm2_g[i]), sh2, sc2), ffn_w_gu[i], ffn_w_down[i])
        if not last:
            ctx = ctx + cg2 * swiglu(modulate(rms_norm(ctx, norm2_g[i]), csh2, csc2), ffn_w_gu[i], ffn_w_down[i])
    return rms_norm(x, final_g)
```

```python
import functools
import math

import jax
import jax.numpy as jnp
from jax import lax
from jax.experimental import pallas as pl
from jax.experimental.pallas import tpu as pltpu

F32 = jnp.float32
BF16 = jnp.bfloat16

GRID_W = 64
EPS = 1e-6
ROPE_THETA = 10000.0
POOL_WINDOWS = (2, 4, 8, 16)
POOL_HALO = 16
SGU_HEADS = 4
SGU_CHUNK = 128
MLA_HEADS = 16
MLA_Q_RANK = 1024
MLA_KV_RANK = 512
MLA_NOPE = 128
MLA_ROPE = 64
MLA_V = 128
GQA_HEADS = 16
GQA_KV_HEADS = 4
GQA_HEAD_DIM = 128
LANE = 128
MOD_ROWS = 16

VMEM_LIMIT = 56 * 1024 * 1024


def _params(n_axes):
    return pltpu.CompilerParams(dimension_semantics=("arbitrary",) * n_axes, vmem_limit_bytes=VMEM_LIMIT)


def _silu(x):
    return x / (1.0 + jnp.exp(-x))


def _gelu_tanh(x):
    return 0.5 * x * (1.0 + jnp.tanh(math.sqrt(2.0 / math.pi) * (x + 0.044715 * (x * x * x))))


def _rms(x, g):
    return x * lax.rsqrt(jnp.mean(x * x, axis=-1, keepdims=True) + EPS) * g


def _adaln_body(c_ref, w_ref, b_ref, o_ref):
    sc = _silu(c_ref[...]).astype(BF16)
    o_ref[0] = jnp.dot(sc, w_ref[0].astype(BF16), preferred_element_type=F32) + b_ref[0]


def adaln_all(cc, mod_w, mod_b, tn=1024):
    L, D, N = mod_w.shape
    return pl.pallas_call(
        _adaln_body,
        grid=(L, N // tn),
        in_specs=[
            pl.BlockSpec((MOD_ROWS, D), lambda l, j: (0, 0)),
            pl.BlockSpec((1, D, tn), lambda l, j: (l, 0, j)),
            pl.BlockSpec((1, 1, tn), lambda l, j: (l, 0, j)),
        ],
        out_specs=pl.BlockSpec((1, MOD_ROWS, tn), lambda l, j: (l, 0, j)),
        out_shape=jax.ShapeDtypeStruct((L, MOD_ROWS, N), F32),
        compiler_params=_params(2),
        name="adaln",
    )(cc, mod_w, mod_b.reshape(L, 1, N))


def _norm_mod_body(x_ref, g_ref, sh_ref, sc_ref, o_ref):
    y = _rms(x_ref[...], g_ref[...])
    o_ref[...] = (y * (1.0 + sc_ref[0]) + sh_ref[0]).astype(o_ref.dtype)


def _rms_body(x_ref, g_ref, o_ref):
    o_ref[...] = _rms(x_ref[...], g_ref[...]).astype(o_ref.dtype)


class Mods:
    def __init__(self, table):
        self.table = table

    def index(self, layer, which, tm, n_batch, seq):
        n_x = n_batch * seq // tm

        def row(i):
            return jnp.where(i < n_x, (i * tm) // seq, n_batch)

        base = (layer * 6 + which) * MOD_ROWS
        return lambda i: base + row(i)


def norm_mod(t, gain, mods, layer, which_shift, which_scale, dims, tm=512):
    M, D = t.shape
    n_batch, seq = dims
    sh_idx = mods.index(layer, which_shift, tm, n_batch, seq)
    sc_idx = mods.index(layer, which_scale, tm, n_batch, seq)
    return pl.pallas_call(
        _norm_mod_body,
        grid=(M // tm,),
        in_specs=[
            pl.BlockSpec((tm, D), lambda i: (i, 0)),
            pl.BlockSpec((1, D), lambda i: (0, 0)),
            pl.BlockSpec((1, 1, D), lambda i: (sh_idx(i), 0, 0)),
            pl.BlockSpec((1, 1, D), lambda i: (sc_idx(i), 0, 0)),
        ],
        out_specs=pl.BlockSpec((tm, D), lambda i: (i, 0)),
        out_shape=jax.ShapeDtypeStruct((M, D), BF16),
        compiler_params=_params(1),
        name="norm_mod",
    )(t, gain.reshape(1, D), mods.table, mods.table)


def final_norm(t, gain, m_rows, tm=512):
    D = t.shape[1]
    return pl.pallas_call(
        _rms_body,
        grid=(m_rows // tm,),
        in_specs=[pl.BlockSpec((tm, D), lambda i: (i, 0)), pl.BlockSpec((1, D), lambda i: (0, 0))],
        out_specs=pl.BlockSpec((tm, D), lambda i: (i, 0)),
        out_shape=jax.ShapeDtypeStruct((m_rows, D), F32),
        compiler_params=_params(1),
        name="final_norm",
    )(t, gain.reshape(1, D))


def _mm_body(a_ref, w_ref, o_ref):
    o_ref[...] = jnp.dot(a_ref[...], w_ref[...], preferred_element_type=F32).astype(o_ref.dtype)


def matmul(a, w, m_rows, out_dtype, tm=1024, tn=512, name="mm"):
    K, N = w.shape
    return pl.pallas_call(
        _mm_body,
        grid=(m_rows // tm, N // tn),
        in_specs=[pl.BlockSpec((tm, K), lambda i, j: (i, 0)), pl.BlockSpec((K, tn), lambda i, j: (0, j))],
        out_specs=pl.BlockSpec((tm, tn), lambda i, j: (i, j)),
        out_shape=jax.ShapeDtypeStruct((m_rows, N), out_dtype),
        compiler_params=_params(2),
        name=name,
    )(a, w)


def _mm_res_body(a_ref, w_ref, r_ref, g_ref, o_ref):
    acc = jnp.dot(a_ref[...], w_ref[...], preferred_element_type=F32)
    o_ref[...] = r_ref[...] + g_ref[0] * acc


def matmul_residual(a, w, res, mods, layer, which_gate, m_rows, dims, tm=1024, tn=512, name="mm_res"):
    K, N = w.shape
    n_batch, seq = dims
    g_idx = mods.index(layer, which_gate, tm, n_batch, seq)
    return pl.pallas_call(
        _mm_res_body,
        grid=(m_rows // tm, N // tn),
        in_specs=[
            pl.BlockSpec((tm, K), lambda i, j: (i, 0)),
            pl.BlockSpec((K, tn), lambda i, j: (0, j)),
            pl.BlockSpec((tm, tn), lambda i, j: (i, j)),
            pl.BlockSpec((1, 1, tn), lambda i, j: (g_idx(i), 0, j)),
        ],
        out_specs=pl.BlockSpec((tm, tn), lambda i, j: (i, j)),
        out_shape=jax.ShapeDtypeStruct((m_rows, N), F32),
        compiler_params=_params(2),
        name=name,
    )(a, w, res, mods.table)


def _swiglu_body(a_ref, wg_ref, wu_ref, o_ref):
    a = a_ref[...]
    g = jnp.dot(a, wg_ref[...], preferred_element_type=F32)
    u = jnp.dot(a, wu_ref[...], preferred_element_type=F32)
    o_ref[...] = (_silu(g) * u).astype(o_ref.dtype)


def matmul_swiglu(a, w_gu, m_rows, tm=1024, tn=256):
    K, N2 = w_gu.shape
    H = N2 // 2
    nj = H // tn
    return pl.pallas_call(
        _swiglu_body,
        grid=(m_rows // tm, nj),
        in_specs=[
            pl.BlockSpec((tm, K), lambda i, j: (i, 0)),
            pl.BlockSpec((K, tn), lambda i, j: (0, j)),
            pl.BlockSpec((K, tn), lambda i, j: (0, j + nj)),
        ],
        out_specs=pl.BlockSpec((tm, tn), lambda i, j: (i, j)),
        out_shape=jax.ShapeDtypeStruct((m_rows, H), BF16),
        compiler_params=_params(2),
        name="ffn_up",
    )(a, w_gu, w_gu)


def _pool_body(z_ref, w_ref, s_ref, o_ref, *, seq):
    g = pl.program_id(1)
    z = z_ref[...]
    C = z.shape[1]
    padded = seq + POOL_HALO
    zp = jnp.concatenate([z, jnp.zeros((POOL_HALO, C), F32)], axis=0)

    def prev(a, k):
        return pltpu.roll(a, k, 0)

    def nxt(a, k):
        return pltpu.roll(a, padded - k, 0)

    t = lax.broadcasted_iota(jnp.int32, (seq, C), 0)

    def finish(win_sum, window):
        half = window // 2
        cnt = (jnp.minimum(t - half + window, seq) - jnp.maximum(t - half, 0)).astype(F32)
        pooled = win_sum[:seq] / cnt - z
        mixed = jnp.dot(pooled.astype(BF16), w_ref[0].astype(BF16), preferred_element_type=F32)
        o_ref[...] = (mixed * s_ref[...]).astype(o_ref.dtype)

    def window_sum(level):
        s = zp + prev(zp, 1)
        for lv in range(level):
            s = prev(s, 2**lv) + nxt(s, 2**lv)
        return s

    for gi, window in enumerate(POOL_WINDOWS):

        @pl.when(g == gi)
        def _(gi=gi, window=window):
            finish(window_sum(gi), window)


def pool_mixer(z, pool_w, pool_scale, seq, row_block0, n_seq):
    n_groups, G, _ = pool_w.shape
    return pl.pallas_call(
        functools.partial(_pool_body, seq=seq),
        grid=(n_seq, n_groups),
        in_specs=[
            pl.BlockSpec((seq, G), lambda b, g: (row_block0 + b, g)),
            pl.BlockSpec((1, G, G), lambda b, g: (g, 0, 0)),
            pl.BlockSpec((1, G), lambda b, g: (0, g)),
        ],
        out_specs=pl.BlockSpec((seq, G), lambda b, g: (b, g)),
        out_shape=jax.ShapeDtypeStruct((n_seq * seq, n_groups * G), BF16),
        compiler_params=_params(2),
        name="pool_mixer",
    )(z, pool_w, pool_scale.reshape(1, n_groups * G))


def _sgu_body(u_ref, v_ref, ng_ref, ws_ref, bs_ref, o_ref):
    ws = ws_ref[0].astype(BF16)
    bs = bs_ref[0]
    ng = ng_ref[0]
    for c in range(u_ref.shape[0] // SGU_CHUNK):
        rows = pl.ds(c * SGU_CHUNK, SGU_CHUNK)
        vn = _rms(_gelu_tanh(v_ref[rows, :]), ng)
        gate = jnp.dot(ws, vn.astype(BF16), preferred_element_type=F32) + bs
        o_ref[rows, :] = (_gelu_tanh(u_ref[rows, :]) * gate).astype(o_ref.dtype)


def sgu_mixer(z, norm_g, w_s, b_s, col0, ts=1024):
    M = z.shape[0]
    H, P, _ = w_s.shape
    hd = norm_g.shape[1]
    cb0 = col0 // hd
    return pl.pallas_call(
        _sgu_body,
        grid=(M // ts, H),
        in_specs=[
            pl.BlockSpec((ts, hd), lambda i, h: (i, cb0 + h)),
            pl.BlockSpec((ts, hd), lambda i, h: (i, cb0 + H + h)),
            pl.BlockSpec((1, 1, hd), lambda i, h: (h, 0, 0)),
            pl.BlockSpec((1, P, P), lambda i, h: (h, 0, 0)),
            pl.BlockSpec((1, P, 1), lambda i, h: (h, 0, 0)),
        ],
        out_specs=pl.BlockSpec((ts, hd), lambda i, h: (i, h)),
        out_shape=jax.ShapeDtypeStruct((M, H * hd), BF16),
        compiler_params=_params(2),
        name="sgu_mixer",
    )(z, z, norm_g.reshape(H, 1, hd), w_s, b_s.reshape(H, P, 1))


def _rope_half(x, cos, sin_signed):
    return x * cos + pltpu.roll(x, LANE // 2, 1) * sin_signed


def _prep_q_body(zq_ref, qag_ref, qng_ref, cos_ref, sin_ref, cq_ref, gq_ref):
    cq_ref[...] = _rms(zq_ref[:, :MLA_Q_RANK], qag_ref[...]).astype(cq_ref.dtype)
    cos = cos_ref[...]
    sin = sin_ref[...]
    g = qng_ref[...]
    for h in range(GQA_HEADS):
        x = _rms(zq_ref[:, MLA_Q_RANK + h * LANE : MLA_Q_RANK + (h + 1) * LANE], g)
        gq_ref[:, h * LANE : (h + 1) * LANE] = _rope_half(x, cos, sin).astype(gq_ref.dtype)


def prep_q(zq, q_a_g, q_norm_g, cos_g, sin_g, tm=512):
    M, N = zq.shape
    nq = GQA_HEADS * GQA_HEAD_DIM
    return pl.pallas_call(
        _prep_q_body,
        grid=(M // tm,),
        in_specs=[
            pl.BlockSpec((tm, N), lambda i: (i, 0)),
            pl.BlockSpec((1, MLA_Q_RANK), lambda i: (0, 0)),
            pl.BlockSpec((1, LANE), lambda i: (0, 0)),
            pl.BlockSpec((tm, LANE), lambda i: (i, 0)),
            pl.BlockSpec((tm, LANE), lambda i: (i, 0)),
        ],
        out_specs=[pl.BlockSpec((tm, MLA_Q_RANK), lambda i: (i, 0)), pl.BlockSpec((tm, nq), lambda i: (i, 0))],
        out_shape=[jax.ShapeDtypeStruct((M, MLA_Q_RANK), BF16), jax.ShapeDtypeStruct((M, nq), BF16)],
        compiler_params=_params(1),
        name="prep_q",
    )(zq, q_a_g.reshape(1, -1), q_norm_g.reshape(1, -1), cos_g, sin_g)


def _prep_kv_body(z_ref, kag_ref, kng_ref, cg_ref, sg_ref, cm_ref, sm_ref, ckv_ref, kr_ref, gk_ref, gv_ref):
    ckv_ref[...] = _rms(z_ref[:, :MLA_KV_RANK], kag_ref[...]).astype(ckv_ref.dtype)
    kr = z_ref[:, MLA_KV_RANK : MLA_KV_RANK + LANE]
    lane = lax.broadcasted_iota(jnp.int32, kr.shape, 1)
    half = MLA_ROPE // 2
    swapped = jnp.where(lane < half, pltpu.roll(kr, LANE - half, 1), pltpu.roll(kr, half, 1))
    kr_ref[...] = (kr * cm_ref[...] + swapped * sm_ref[...]).astype(kr_ref.dtype)
    off_k = MLA_KV_RANK + LANE
    off_v = off_k + GQA_KV_HEADS * LANE
    cos = cg_ref[...]
    sin = sg_ref[...]
    g = kng_ref[...]
    for h in range(GQA_KV_HEADS):
        x = _rms(z_ref[:, off_k + h * LANE : off_k + (h + 1) * LANE], g)
        gk_ref[:, h * LANE : (h + 1) * LANE] = _rope_half(x, cos, sin).astype(gk_ref.dtype)
    gv_ref[...] = z_ref[:, off_v : off_v + GQA_KV_HEADS * LANE].astype(gv_ref.dtype)


def prep_kv(zkv, kv_a_g, k_norm_g, cos_g, sin_g, cos_m, sin_m, tm=512):
    M, N = zkv.shape
    nk = GQA_KV_HEADS * GQA_HEAD_DIM
    tok = lambda i: (i, 0)
    fixed = lambda i: (0, 0)
    return pl.pallas_call(
        _prep_kv_body,
        grid=(M // tm,),
        in_specs=[
            pl.BlockSpec((tm, N), tok),
            pl.BlockSpec((1, MLA_KV_RANK), fixed),
            pl.BlockSpec((1, LANE), fixed),
            pl.BlockSpec((tm, LANE), tok),
            pl.BlockSpec((tm, LANE), tok),
            pl.BlockSpec((tm, LANE), tok),
            pl.BlockSpec((tm, LANE), tok),
        ],
        out_specs=[
            pl.BlockSpec((tm, MLA_KV_RANK), tok),
            pl.BlockSpec((tm, LANE), tok),
            pl.BlockSpec((tm, nk), tok),
            pl.BlockSpec((tm, nk), tok),
        ],
        out_shape=[
            jax.ShapeDtypeStruct((M, MLA_KV_RANK), BF16),
            jax.ShapeDtypeStruct((M, LANE), BF16),
            jax.ShapeDtypeStruct((M, nk), BF16),
            jax.ShapeDtypeStruct((M, nk), BF16),
        ],
        compiler_params=_params(1),
        name="prep_kv",
    )(zkv, kv_a_g.reshape(1, -1), k_norm_g.reshape(1, -1), cos_g, sin_g, cos_m, sin_m)


def _mm_qrope_body(a_ref, w_ref, c_ref, s_ref, o_ref):
    acc = jnp.dot(a_ref[...], w_ref[...], preferred_element_type=F32)
    cos = c_ref[...]
    sin = s_ref[...]
    for h in range(acc.shape[1] // (2 * LANE)):
        lo = h * 2 * LANE
        o_ref[:, lo : lo + LANE] = acc[:, lo : lo + LANE].astype(o_ref.dtype)
        t = acc[:, lo + LANE : lo + 2 * LANE]
        o_ref[:, lo + LANE : lo + 2 * LANE] = (t * cos + pltpu.roll(t, LANE // 2, 1) * sin).astype(o_ref.dtype)


def matmul_qrope(a, w, cos_m, sin_m_abs, tm=1024, tn=1024):
    M, K = a.shape
    N = w.shape[1]
    return pl.pallas_call(
        _mm_qrope_body,
        grid=(M // tm, N // tn),
        in_specs=[
            pl.BlockSpec((tm, K), lambda i, j: (i, 0)),
            pl.BlockSpec((K, tn), lambda i, j: (0, j)),
            pl.BlockSpec((tm, LANE), lambda i, j: (i, 0)),
            pl.BlockSpec((tm, LANE), lambda i, j: (i, 0)),
        ],
        out_specs=pl.BlockSpec((tm, tn), lambda i, j: (i, j)),
        out_shape=jax.ShapeDtypeStruct((M, N), BF16),
        compiler_params=_params(2),
        name="mla_q_up",
    )(a, w, cos_m, sin_m_abs)


def _softmax_pv(s_c, s_x, vc, vx, o_ref):
    m = jnp.maximum(jnp.max(s_c, axis=-1, keepdims=True), jnp.max(s_x, axis=-1, keepdims=True))
    p_c = jnp.exp(s_c - m)
    p_x = jnp.exp(s_x - m)
    inv = 1.0 / (jnp.sum(p_c, axis=-1, keepdims=True) + jnp.sum(p_x, axis=-1, keepdims=True))
    o = jnp.dot((p_c * inv).astype(BF16), vc, preferred_element_type=F32)
    o = o + jnp.dot((p_x * inv).astype(BF16), vx, preferred_element_type=F32)
    o_ref[...] = o.astype(o_ref.dtype)


_NT = (((1,), (1,)), ((), ()))


def _mla_body(q_ref, knc_ref, knx_ref, krc_ref, krx_ref, vc_ref, vx_ref, o_ref, *, scale):
    q = q_ref[...]
    kc = jnp.concatenate([knc_ref[...], krc_ref[...]], axis=1)
    kx = jnp.concatenate([knx_ref[...], krx_ref[...]], axis=1)
    s_c = lax.dot_general(q, kc, _NT, preferred_element_type=F32) * scale
    s_x = lax.dot_general(q, kx, _NT, preferred_element_type=F32) * scale
    _softmax_pv(s_c, s_x, vc_ref[...], vx_ref[...], o_ref)


def mla_attention(q, kv, kr, n_batch, seq, ctx_len, tq=512):
    nq = seq // tq
    cb0 = n_batch * seq // ctx_len
    scale = 1.0 / math.sqrt(MLA_NOPE + MLA_ROPE)
    return pl.pallas_call(
        functools.partial(_mla_body, scale=scale),
        grid=(n_batch, MLA_HEADS, nq),
        in_specs=[
            pl.BlockSpec((tq, 2 * LANE), lambda b, h, i: (b * nq + i, h)),
            pl.BlockSpec((ctx_len, LANE), lambda b, h, i: (cb0 + b, 2 * h)),
            pl.BlockSpec((seq, LANE), lambda b, h, i: (b, 2 * h)),
            pl.BlockSpec((ctx_len, LANE), lambda b, h, i: (cb0 + b, 0)),
            pl.BlockSpec((seq, LANE), lambda b, h, i: (b, 0)),
            pl.BlockSpec((ctx_len, LANE), lambda b, h, i: (cb0 + b, 2 * h + 1)),
            pl.BlockSpec((seq, LANE), lambda b, h, i: (b, 2 * h + 1)),
        ],
        out_specs=pl.BlockSpec((tq, LANE), lambda b, h, i: (b * nq + i, h)),
        out_shape=jax.ShapeDtypeStruct((n_batch * seq, MLA_HEADS * MLA_V), BF16),
        compiler_params=_params(3),
        name="mla_attention",
    )(q, kv, kv, kr, kr, kv, kv)


def _gqa_body(q_ref, kc_ref, kx_ref, vc_ref, vx_ref, o_ref, *, scale):
    q = q_ref[...]
    s_c = lax.dot_general(q, kc_ref[...], _NT, preferred_element_type=F32) * scale
    s_x = lax.dot_general(q, kx_ref[...], _NT, preferred_element_type=F32) * scale
    _softmax_pv(s_c, s_x, vc_ref[...], vx_ref[...], o_ref)


def gqa_attention(q, k, v, n_batch, seq, ctx_len, tq=512):
    nq = seq // tq
    cb0 = n_batch * seq // ctx_len
    rep = GQA_HEADS // GQA_KV_HEADS
    scale = 1.0 / math.sqrt(GQA_HEAD_DIM)
    return pl.pallas_call(
        functools.partial(_gqa_body, scale=scale),
        grid=(n_batch, GQA_HEADS, nq),
        in_specs=[
            pl.BlockSpec((tq, LANE), lambda b, h, i: (b * nq + i, h)),
            pl.BlockSpec((ctx_len, LANE), lambda b, h, i: (cb0 + b, h // rep)),
            pl.BlockSpec((seq, LANE), lambda b, h, i: (b, h // rep)),
            pl.BlockSpec((ctx_len, LANE), lambda b, h, i: (cb0 + b, h // rep)),
            pl.BlockSpec((seq, LANE), lambda b, h, i: (b, h // rep)),
        ],
        out_specs=pl.BlockSpec((tq, LANE), lambda b, h, i: (b * nq + i, h)),
        out_shape=jax.ShapeDtypeStruct((n_batch * seq, GQA_HEADS * GQA_HEAD_DIM), BF16),
        compiler_params=_params(3),
        name="gqa_attention",
    )(q, k, k, v, v)


def _rope_tables(n_batch, seq, ctx_len):
    rows = seq // GRID_W
    row = jnp.repeat(jnp.arange(rows, dtype=F32), GRID_W)
    col = jnp.tile(jnp.arange(GRID_W, dtype=F32), rows)

    def cos_sin(rot_dim):
        n_freq = rot_dim // 4
        inv = ROPE_THETA ** (-jnp.arange(n_freq, dtype=F32) / n_freq)
        ang = jnp.concatenate([row[:, None] * inv, col[:, None] * inv], axis=-1)
        return jnp.cos(ang), jnp.sin(ang)

    def over_tokens(x_part, ctx_value):
        x_rows = jnp.tile(x_part, (n_batch, 1))
        c_rows = jnp.broadcast_to(ctx_value, (n_batch * ctx_len, x_part.shape[1]))
        return jnp.concatenate([x_rows, c_rows], axis=0)

    cg, sg = cos_sin(GQA_HEAD_DIM)
    cos_g = over_tokens(jnp.concatenate([cg, cg], axis=1), jnp.ones((1, LANE), F32))
    sin_g = over_tokens(jnp.concatenate([-sg, sg], axis=1), jnp.zeros((1, LANE), F32))
    cm, sm = cos_sin(MLA_ROPE)
    zpad = jnp.zeros((seq, LANE - MLA_ROPE), F32)
    ident = jnp.concatenate([jnp.ones((1, MLA_ROPE), F32), jnp.zeros((1, LANE - MLA_ROPE), F32)], axis=1)
    cos_m = over_tokens(jnp.concatenate([cm, cm, zpad], axis=1), ident)
    sin_m = over_tokens(jnp.concatenate([-sm, sm, zpad], axis=1), jnp.zeros((1, LANE), F32))
    sin_m_abs = over_tokens(jnp.concatenate([sm, sm, zpad], axis=1), jnp.zeros((1, LANE), F32))
    return cos_g, sin_g, cos_m, sin_m, sin_m_abs


def _mla_q_weight(w_q_b):
    r = w_q_b.shape[0]
    w = w_q_b.reshape(r, MLA_HEADS, MLA_NOPE + MLA_ROPE)
    nope, rope = w[..., :MLA_NOPE], w[..., MLA_NOPE:]
    x1, x2 = rope[..., : MLA_ROPE // 2], rope[..., MLA_ROPE // 2 :]
    return jnp.concatenate([nope, rope, -x2, x1], axis=-1).reshape(r, MLA_HEADS * 2 * LANE).astype(BF16)


def _odd_kv_weight(w_in, q_cols):
    w = w_in[:, q_cols:]
    pad = jnp.zeros((w.shape[0], LANE - MLA_ROPE), w.dtype)
    split = MLA_KV_RANK + MLA_ROPE
    return jnp.concatenate([w[:, :split], pad, w[:, split:]], axis=1).astype(BF16)


def kernel(x, c, ctx, c_ctx, mod_w, mod_b, norm1_g, norm2_g, final_g, ffn_w_gu, ffn_w_down, ev_w_in, ev_pool_w,
           ev_pool_scale, ev_sgu_norm_g, ev_sgu_w_s, ev_sgu_b, ev_w_out, od_w_in, od_q_a_g, od_w_q_b, od_kv_a_g,
           od_w_kv_b, od_q_norm_g, od_k_norm_g, od_w_out):
    B, S, D = x.shape
    CL = ctx.shape[1]
    depth = mod_w.shape[0]
    assert depth == 2, "layer schedule below is written for one even and one odd layer"
    n_x = B * S
    n_all = n_x + B * CL
    dims = (B, S)

    t = jnp.concatenate([x.reshape(n_x, D), ctx.reshape(B * CL, D)], axis=0)
    cc = jnp.concatenate([c, c_ctx[None, :], jnp.zeros((MOD_ROWS - B - 1, D), F32)], axis=0)
    m = adaln_all(cc, mod_w, mod_b)
    table = m.reshape(depth, MOD_ROWS, 6, D).transpose(0, 2, 1, 3).reshape(depth * 6 * MOD_ROWS, 1, D)
    mods = Mods(table)
    SH1, SC1, G1, SH2, SC2, G2 = range(6)

    def ffn(t_in, layer, m_rows):
        h = norm_mod(t_in, norm2_g[layer], mods, layer, SH2, SC2, dims)
        act = matmul_swiglu(h, ffn_w_gu[layer].astype(BF16), m_rows)
        return matmul_residual(act, ffn_w_down[layer].astype(BF16), t_in, mods, layer, G2, m_rows, dims,
                               tm=512, tn=512, name="ffn_down")

    pool_width = ev_pool_scale.shape[1]
    h = norm_mod(t, norm1_g[0], mods, 0, SH1, SC1, dims)
    z = matmul(h, ev_w_in[0].astype(BF16), n_all, F32, name="even_in")
    ya_x = pool_mixer(z, ev_pool_w[0], ev_pool_scale[0], S, 0, B)
    ya_c = pool_mixer(z, ev_pool_w[0], ev_pool_scale[0], CL, n_x // CL, B)
    yb = sgu_mixer(z, ev_sgu_norm_g[0], ev_sgu_w_s[0], ev_sgu_b[0], pool_width)
    y = jnp.concatenate([jnp.concatenate([ya_x, ya_c], axis=0), yb], axis=1)
    t = matmul_residual(y, ev_w_out[0].astype(BF16), t, mods, 0, G1, n_all, dims, name="even_out")
    t = ffn(t, 0, n_all)

    q_cols = MLA_Q_RANK + GQA_HEADS * GQA_HEAD_DIM
    cos_g, sin_g, cos_m, sin_m, sin_m_abs = _rope_tables(B, S, CL)
    h = norm_mod(t, norm1_g[1], mods, 1, SH1, SC1, dims)
    zq = matmul(h, od_w_in[0][:, :q_cols].astype(BF16), n_x, F32, name="odd_in_q")
    w_kv = _odd_kv_weight(od_w_in[0], q_cols)
    zkv = matmul(h, w_kv, n_all, F32, tm=512, tn=w_kv.shape[1], name="odd_in_kv")
    cq, gq = prep_q(zq, od_q_a_g[0], od_q_norm_g[0], cos_g, sin_g)
    ckv, kr, gk, gv = prep_kv(zkv, od_kv_a_g[0], od_k_norm_g[0], cos_g, sin_g, cos_m, sin_m)
    q_m = matmul_qrope(cq, _mla_q_weight(od_w_q_b[0]), cos_m, sin_m_abs)
    kv_m = matmul(ckv, od_w_kv_b[0].astype(BF16), n_all, BF16, tn=1024, name="mla_kv_up")
    o_m = mla_attention(q_m, kv_m, kr, B, S, CL)
    o_g = gqa_attention(gq, gk, gv, B, S, CL)
    o = jnp.concatenate([o_m, o_g], axis=1)
    t = matmul_residual(o, od_w_out[0].astype(BF16), t, mods, 1, G1, n_x, dims, name="odd_out")
    t = ffn(t, 1, n_x)

    return final_norm(t, final_g, n_x).reshape(B, S, D)
```

```python
import functools
import math

import jax
import jax.numpy as jnp
from jax import lax
from jax.experimental import pallas as pl
from jax.experimental.pallas import tpu as pltpu

F32 = jnp.float32
BF16 = jnp.bfloat16

GRID_W = 64
EPS = 1e-6
ROPE_THETA = 10000.0
POOL_WINDOWS = (2, 4, 8, 16)
POOL_HALO = 16
SGU_CHUNK = 128
MLA_HEADS = 16
MLA_Q_RANK = 1024
MLA_KV_RANK = 512
MLA_NOPE = 128
MLA_ROPE = 64
MLA_V = 128
GQA_HEADS = 16
GQA_KV_HEADS = 4
GQA_HEAD_DIM = 128
LANE = 128
MOD_ROWS = 16
SH1, SC1, G1, SH2, SC2, G2 = range(6)

VMEM_LIMIT = 56 * 1024 * 1024


def _params(n_axes):
    return pltpu.CompilerParams(dimension_semantics=("arbitrary",) * n_axes, vmem_limit_bytes=VMEM_LIMIT)


def _silu(x):
    return x / (1.0 + jnp.exp(-x))


def _gelu_tanh(x):
    return 0.5 * x * (1.0 + jnp.tanh(math.sqrt(2.0 / math.pi) * (x + 0.044715 * (x * x * x))))


def _rms(x, g):
    return x * lax.rsqrt(jnp.mean(x * x, axis=-1, keepdims=True) + EPS) * g


def _row_sources(srcs, tm, width, col_of):
    n0 = srcs[0].shape[0] // tm
    if len(srcs) == 1:
        return n0, [pl.BlockSpec((tm, width), lambda i, *r: (i, col_of(*r)))]
    n1 = srcs[1].shape[0] // tm
    return n0, [
        pl.BlockSpec((tm, width), lambda i, *r: (jnp.minimum(i, n0 - 1), col_of(*r))),
        pl.BlockSpec((tm, width), lambda i, *r: (jnp.clip(i - n0, 0, n1 - 1), col_of(*r))),
    ]


def _for_source(refs, n0, i, emit):
    if len(refs) == 1:
        emit(refs[0])
    else:
        pl.when(i < n0)(lambda: emit(refs[0]))
        pl.when(i >= n0)(lambda: emit(refs[1]))


def _adaln_body(c_ref, w_ref, b_ref, o_ref):
    sc = _silu(c_ref[...]).astype(BF16)
    o_ref[0] = jnp.dot(sc, w_ref[0].astype(BF16), preferred_element_type=F32) + b_ref[0]


def adaln_all(cc, mod_w, mod_b, tn=1024):
    L, D, N = mod_w.shape
    return pl.pallas_call(
        _adaln_body,
        grid=(L, N // tn),
        in_specs=[
            pl.BlockSpec((MOD_ROWS, D), lambda l, j: (0, 0)),
            pl.BlockSpec((1, D, tn), lambda l, j: (l, 0, j)),
            pl.BlockSpec((1, 1, tn), lambda l, j: (l, 0, j)),
        ],
        out_specs=pl.BlockSpec((1, MOD_ROWS, tn), lambda l, j: (l, 0, j)),
        out_shape=jax.ShapeDtypeStruct((L, MOD_ROWS, N), F32),
        compiler_params=_params(2),
        name="adaln",
    )(cc, mod_w, mod_b.reshape(L, 1, N))


class Mods:
    def __init__(self, table, n_batch, seq):
        self.table = table
        self.n_batch = n_batch
        self.seq = seq

    def index(self, layer, which, tm):
        n_x = self.n_batch * self.seq // tm
        base = (layer * 6 + which) * MOD_ROWS
        return lambda i: base + jnp.where(i < n_x, (i * tm) // self.seq, self.n_batch)


def _norm_mod_body(*refs, n_src, n0):
    g_ref, sh_ref, sc_ref, o_ref = refs[n_src:]

    def emit(x_ref):
        y = _rms(x_ref[...], g_ref[...])
        o_ref[...] = (y * (1.0 + sc_ref[0]) + sh_ref[0]).astype(o_ref.dtype)

    _for_source(refs[:n_src], n0, pl.program_id(0), emit)


def norm_mod(srcs, gain, mods, layer, which_shift, which_scale, tm=512):
    D = srcs[0].shape[1]
    M = sum(s.shape[0] for s in srcs)
    sh_idx = mods.index(layer, which_shift, tm)
    sc_idx = mods.index(layer, which_scale, tm)
    n0, src_specs = _row_sources(srcs, tm, D, lambda: 0)
    return pl.pallas_call(
        functools.partial(_norm_mod_body, n_src=len(srcs), n0=n0),
        grid=(M // tm,),
        in_specs=src_specs
        + [
            pl.BlockSpec((1, D), lambda i: (0, 0)),
            pl.BlockSpec((1, 1, D), lambda i: (sh_idx(i), 0, 0)),
            pl.BlockSpec((1, 1, D), lambda i: (sc_idx(i), 0, 0)),
        ],
        out_specs=pl.BlockSpec((tm, D), lambda i: (i, 0)),
        out_shape=jax.ShapeDtypeStruct((M, D), BF16),
        compiler_params=_params(1),
        name="norm_mod",
    )(*srcs, gain.reshape(1, D), mods.table, mods.table)


def _rms_body(x_ref, g_ref, o_ref):
    o_ref[...] = _rms(x_ref[...], g_ref[...]).astype(o_ref.dtype)


def final_norm(t, gain, m_rows, tm=512):
    D = t.shape[1]
    return pl.pallas_call(
        _rms_body,
        grid=(m_rows // tm,),
        in_specs=[pl.BlockSpec((tm, D), lambda i: (i, 0)), pl.BlockSpec((1, D), lambda i: (0, 0))],
        out_specs=pl.BlockSpec((tm, D), lambda i: (i, 0)),
        out_shape=jax.ShapeDtypeStruct((m_rows, D), F32),
        compiler_params=_params(1),
        name="final_norm",
    )(t, gain.reshape(1, D))


def _mm_body(a_ref, w_ref, o_ref):
    o_ref[...] = jnp.dot(a_ref[...], w_ref[...], preferred_element_type=F32).astype(o_ref.dtype)


def matmul(a, w, wl, m_rows, n_cols, out_dtype, tm=1024, tn=512, name="mm"):
    K = w.shape[1]
    return pl.pallas_call(
        _mm_body,
        grid=(m_rows // tm, n_cols // tn),
        in_specs=[
            pl.BlockSpec((tm, K), lambda i, j: (i, 0)),
            pl.BlockSpec((None, K, tn), lambda i, j: (wl, 0, j)),
        ],
        out_specs=pl.BlockSpec((tm, tn), lambda i, j: (i, j)),
        out_shape=jax.ShapeDtypeStruct((m_rows, n_cols), out_dtype),
        compiler_params=_params(2),
        name=name,
    )(a, w)


def _mm_res_body(*refs, n_a, n_res, n0):
    a_refs, w_refs = refs[:n_a], refs[n_a : 2 * n_a]
    r_refs = refs[2 * n_a : 2 * n_a + n_res]
    g_ref, o_ref = refs[2 * n_a + n_res :]
    acc = jnp.dot(a_refs[0][...], w_refs[0][...], preferred_element_type=F32)
    for a_ref, w_ref in zip(a_refs[1:], w_refs[1:]):
        acc = acc + jnp.dot(a_ref[...], w_ref[...], preferred_element_type=F32)
    upd = g_ref[0] * acc

    def emit(r_ref):
        o_ref[...] = r_ref[...] + upd

    _for_source(r_refs, n0, pl.program_id(0), emit)


def matmul_residual(a_parts, w, wl, res_srcs, mods, layer, which_gate, m_rows, tm=1024, tn=512, name="mm_res"):
    K, N = w.shape[1:]
    kp = K // len(a_parts)
    g_idx = mods.index(layer, which_gate, tm)
    n0, res_specs = _row_sources(res_srcs, tm, tn, lambda j: j)
    a_specs = [pl.BlockSpec((tm, kp), lambda i, j: (i, 0)) for _ in a_parts]
    w_specs = [pl.BlockSpec((None, kp, tn), lambda i, j, p=p: (wl, p, j)) for p in range(len(a_parts))]
    return pl.pallas_call(
        functools.partial(_mm_res_body, n_a=len(a_parts), n_res=len(res_srcs), n0=n0),
        grid=(m_rows // tm, N // tn),
        in_specs=a_specs + w_specs + res_specs + [pl.BlockSpec((1, 1, tn), lambda i, j: (g_idx(i), 0, j))],
        out_specs=pl.BlockSpec((tm, tn), lambda i, j: (i, j)),
        out_shape=jax.ShapeDtypeStruct((m_rows, N), F32),
        compiler_params=_params(2),
        name=name,
    )(*a_parts, *([w] * len(a_parts)), *res_srcs, mods.table)


def _swiglu_body(a_ref, wg_ref, wu_ref, o_ref):
    a = a_ref[...]
    g = jnp.dot(a, wg_ref[...], preferred_element_type=F32)
    u = jnp.dot(a, wu_ref[...], preferred_element_type=F32)
    o_ref[...] = (_silu(g) * u).astype(o_ref.dtype)


def matmul_swiglu(a, w_gu, wl, m_rows, tm=2048, tn=256):
    K, N2 = w_gu.shape[1:]
    H = N2 // 2
    nj = H // tn
    return pl.pallas_call(
        _swiglu_body,
        grid=(m_rows // tm, nj),
        in_specs=[
            pl.BlockSpec((tm, K), lambda i, j: (i, 0)),
            pl.BlockSpec((None, K, tn), lambda i, j: (wl, 0, j)),
            pl.BlockSpec((None, K, tn), lambda i, j: (wl, 0, j + nj)),
        ],
        out_specs=pl.BlockSpec((tm, tn), lambda i, j: (i, j)),
        out_shape=jax.ShapeDtypeStruct((m_rows, H), BF16),
        compiler_params=_params(2),
        name="ffn_up",
    )(a, w_gu, w_gu)


def _pool_body(z_ref, w_ref, s_ref, o_ref, *, n_batch, ctx_len):
    b = pl.program_id(0)
    g = pl.program_id(1)
    rows, C = z_ref.shape

    def run(sub_len):
        n_sub = rows // sub_len
        stride = sub_len + POOL_HALO
        z = z_ref[...]
        gap = jnp.zeros((POOL_HALO, C), F32)
        zp = jnp.concatenate([p for k in range(n_sub) for p in (z[k * sub_len : (k + 1) * sub_len], gap)], axis=0)
        total = n_sub * stride
        t1 = lax.broadcasted_iota(jnp.int32, (sub_len, C), 0)
        t = t1 if n_sub == 1 else jnp.concatenate([t1] * n_sub, axis=0)

        def prev(a, k):
            return pltpu.roll(a, k, 0)

        def nxt(a, k):
            return pltpu.roll(a, total - k, 0)

        def window_sum(level):
            s = zp + prev(zp, 1)
            for lv in range(level):
                s = prev(s, 2**lv) + nxt(s, 2**lv)
            if n_sub == 1:
                return s[:sub_len]
            return jnp.concatenate([s[k * stride : k * stride + sub_len] for k in range(n_sub)], axis=0)

        for gi, window in enumerate(POOL_WINDOWS):

            @pl.when(g == gi)
            def _(gi=gi, window=window):
                half = window // 2
                cnt = (jnp.minimum(t - half + window, sub_len) - jnp.maximum(t - half, 0)).astype(F32)
                pooled = window_sum(gi) / cnt - z
                mixed = jnp.dot(pooled.astype(BF16), w_ref[0].astype(BF16), preferred_element_type=F32)
                o_ref[...] = (mixed * s_ref[...]).astype(o_ref.dtype)

    pl.when(b < n_batch)(lambda: run(rows))
    pl.when(b >= n_batch)(lambda: run(ctx_len))


def pool_mixer(z, pool_w, pool_scale, n_batch, seq, ctx_len):
    M = z.shape[0]
    n_groups, G, _ = pool_w.shape
    assert (M - n_batch * seq) % seq == 0 and seq % ctx_len == 0
    return pl.pallas_call(
        functools.partial(_pool_body, n_batch=n_batch, ctx_len=ctx_len),
        grid=(M // seq, n_groups),
        in_specs=[
            pl.BlockSpec((seq, G), lambda b, g: (b, g)),
            pl.BlockSpec((1, G, G), lambda b, g: (g, 0, 0)),
            pl.BlockSpec((1, G), lambda b, g: (0, g)),
        ],
        out_specs=pl.BlockSpec((seq, G), lambda b, g: (b, g)),
        out_shape=jax.ShapeDtypeStruct((M, n_groups * G), BF16),
        compiler_params=_params(2),
        name="pool_mixer",
    )(z, pool_w, pool_scale.reshape(1, n_groups * G))


def _sgu_body(u_ref, v_ref, ng_ref, ws_ref, bs_ref, o_ref):
    ws = ws_ref[0].astype(BF16)
    bs = bs_ref[0]
    ng = ng_ref[0]
    for c in range(u_ref.shape[0] // SGU_CHUNK):
        rows = pl.ds(c * SGU_CHUNK, SGU_CHUNK)
        vn = _rms(_gelu_tanh(v_ref[rows, :]), ng)
        gate = jnp.dot(ws, vn.astype(BF16), preferred_element_type=F32) + bs
        o_ref[rows, :] = (_gelu_tanh(u_ref[rows, :]) * gate).astype(o_ref.dtype)


def sgu_mixer(z, norm_g, w_s, b_s, col0, ts=1024):
    M = z.shape[0]
    H, P, _ = w_s.shape
    hd = norm_g.shape[1]
    cb0 = col0 // hd
    return pl.pallas_call(
        _sgu_body,
        grid=(M // ts, H),
        in_specs=[
            pl.BlockSpec((ts, hd), lambda i, h: (i, cb0 + h)),
            pl.BlockSpec((ts, hd), lambda i, h: (i, cb0 + H + h)),
            pl.BlockSpec((1, 1, hd), lambda i, h: (h, 0, 0)),
            pl.BlockSpec((1, P, P), lambda i, h: (h, 0, 0)),
            pl.BlockSpec((1, P, 1), lambda i, h: (h, 0, 0)),
        ],
        out_specs=pl.BlockSpec((ts, hd), lambda i, h: (i, h)),
        out_shape=jax.ShapeDtypeStruct((M, H * hd), BF16),
        compiler_params=_params(2),
        name="sgu_mixer",
    )(z, z, norm_g.reshape(H, 1, hd), w_s, b_s.reshape(H, P, 1))


def _rope_half(x, cos, sin_signed):
    return x * cos + pltpu.roll(x, LANE // 2, 1) * sin_signed


def _prep_q_body(zq_ref, qag_ref, qng_ref, cos_ref, sin_ref, cq_ref, gq_ref):
    cq_ref[...] = _rms(zq_ref[:, :MLA_Q_RANK], qag_ref[...]).astype(cq_ref.dtype)
    cos = cos_ref[...]
    sin = sin_ref[...]
    g = qng_ref[...]
    for h in range(GQA_HEADS):
        x = _rms(zq_ref[:, MLA_Q_RANK + h * LANE : MLA_Q_RANK + (h + 1) * LANE], g)
        gq_ref[:, h * LANE : (h + 1) * LANE] = _rope_half(x, cos, sin).astype(gq_ref.dtype)


def prep_q(zq, q_a_g, q_norm_g, cos_g, sin_g, tm=512):
    M, N = zq.shape
    nq = GQA_HEADS * GQA_HEAD_DIM
    return pl.pallas_call(
        _prep_q_body,
        grid=(M // tm,),
        in_specs=[
            pl.BlockSpec((tm, N), lambda i: (i, 0)),
            pl.BlockSpec((1, MLA_Q_RANK), lambda i: (0, 0)),
            pl.BlockSpec((1, LANE), lambda i: (0, 0)),
            pl.BlockSpec((tm, LANE), lambda i: (i, 0)),
            pl.BlockSpec((tm, LANE), lambda i: (i, 0)),
        ],
        out_specs=[pl.BlockSpec((tm, MLA_Q_RANK), lambda i: (i, 0)), pl.BlockSpec((tm, nq), lambda i: (i, 0))],
        out_shape=[jax.ShapeDtypeStruct((M, MLA_Q_RANK), BF16), jax.ShapeDtypeStruct((M, nq), BF16)],
        compiler_params=_params(1),
        name="prep_q",
    )(zq, q_a_g.reshape(1, -1), q_norm_g.reshape(1, -1), cos_g, sin_g)


def _prep_kv_body(z_ref, kag_ref, kng_ref, cg_ref, sg_ref, cm_ref, sm_ref, ckv_ref, kr_ref, gk_ref, gv_ref):
    ckv_ref[...] = _rms(z_ref[:, :MLA_KV_RANK], kag_ref[...]).astype(ckv_ref.dtype)
    kr = z_ref[:, MLA_KV_RANK : MLA_KV_RANK + LANE]
    lane = lax.broadcasted_iota(jnp.int32, kr.shape, 1)
    half = MLA_ROPE // 2
    swapped = jnp.where(lane < half, pltpu.roll(kr, LANE - half, 1), pltpu.roll(kr, half, 1))
    kr_ref[...] = (kr * cm_ref[...] + swapped * sm_ref[...]).astype(kr_ref.dtype)
    off_k = MLA_KV_RANK + LANE
    off_v = off_k + GQA_KV_HEADS * LANE
    cos = cg_ref[...]
    sin = sg_ref[...]
    g = kng_ref[...]
    for h in range(GQA_KV_HEADS):
        x = _rms(z_ref[:, off_k + h * LANE : off_k + (h + 1) * LANE], g)
        gk_ref[:, h * LANE : (h + 1) * LANE] = _rope_half(x, cos, sin).astype(gk_ref.dtype)
    gv_ref[...] = z_ref[:, off_v : off_v + GQA_KV_HEADS * LANE].astype(gv_ref.dtype)


def prep_kv(zkv, kv_a_g, k_norm_g, cos_g, sin_g, cos_m, sin_m, tm=512):
    M, N = zkv.shape
    nk = GQA_KV_HEADS * GQA_HEAD_DIM
    tok = lambda i: (i, 0)
    fixed = lambda i: (0, 0)
    return pl.pallas_call(
        _prep_kv_body,
        grid=(M // tm,),
        in_specs=[
            pl.BlockSpec((tm, N), tok),
            pl.BlockSpec((1, MLA_KV_RANK), fixed),
            pl.BlockSpec((1, LANE), fixed),
            pl.BlockSpec((tm, LANE), tok),
            pl.BlockSpec((tm, LANE), tok),
            pl.BlockSpec((tm, LANE), tok),
            pl.BlockSpec((tm, LANE), tok),
        ],
        out_specs=[
            pl.BlockSpec((tm, MLA_KV_RANK), tok),
            pl.BlockSpec((tm, LANE), tok),
            pl.BlockSpec((tm, nk), tok),
            pl.BlockSpec((tm, nk), tok),
        ],
        out_shape=[
            jax.ShapeDtypeStruct((M, MLA_KV_RANK), BF16),
            jax.ShapeDtypeStruct((M, LANE), BF16),
            jax.ShapeDtypeStruct((M, nk), BF16),
            jax.ShapeDtypeStruct((M, nk), BF16),
        ],
        compiler_params=_params(1),
        name="prep_kv",
    )(zkv, kv_a_g.reshape(1, -1), k_norm_g.reshape(1, -1), cos_g, sin_g, cos_m, sin_m)


def _mm_qrope_body(a_ref, w_ref, c_ref, s_ref, o_ref):
    acc = jnp.dot(a_ref[...], w_ref[...], preferred_element_type=F32)
    cos = c_ref[...]
    sin = s_ref[...]
    for h in range(acc.shape[1] // (2 * LANE)):
        lo = h * 2 * LANE
        o_ref[:, lo : lo + LANE] = acc[:, lo : lo + LANE].astype(o_ref.dtype)
        t = acc[:, lo + LANE : lo + 2 * LANE]
        o_ref[:, lo + LANE : lo + 2 * LANE] = (t * cos + pltpu.roll(t, LANE // 2, 1) * sin).astype(o_ref.dtype)


def matmul_qrope(a, w, cos_m, sin_m_abs, tm=1024, tn=1024):
    M, K = a.shape
    N = w.shape[1]
    return pl.pallas_call(
        _mm_qrope_body,
        grid=(M // tm, N // tn),
        in_specs=[
            pl.BlockSpec((tm, K), lambda i, j: (i, 0)),
            pl.BlockSpec((K, tn), lambda i, j: (0, j)),
            pl.BlockSpec((tm, LANE), lambda i, j: (i, 0)),
            pl.BlockSpec((tm, LANE), lambda i, j: (i, 0)),
        ],
        out_specs=pl.BlockSpec((tm, tn), lambda i, j: (i, j)),
        out_shape=jax.ShapeDtypeStruct((M, N), BF16),
        compiler_params=_params(2),
        name="mla_q_up",
    )(a, w, cos_m, sin_m_abs)


_NT = (((1,), (1,)), ((), ()))
ATTN_SUB = 256


def _attend(q_ref, kc, kx, vc, vx, o_ref, scale):
    c = scale * math.log2(math.e)
    for r in range(q_ref.shape[0] // ATTN_SUB):
        rows = pl.ds(r * ATTN_SUB, ATTN_SUB)
        q = q_ref[rows, :]
        s_c = lax.dot_general(q, kc, _NT, preferred_element_type=F32)
        s_x = lax.dot_general(q, kx, _NT, preferred_element_type=F32)
        m = jnp.maximum(jnp.max(s_c, axis=-1, keepdims=True), jnp.max(s_x, axis=-1, keepdims=True))
        p_c = jnp.exp2((s_c - m) * c)
        p_x = jnp.exp2((s_x - m) * c)
        denom = jnp.sum(p_c, axis=-1, keepdims=True) + jnp.sum(p_x, axis=-1, keepdims=True)
        o = jnp.dot(p_c.astype(BF16), vc, preferred_element_type=F32)
        o = o + jnp.dot(p_x.astype(BF16), vx, preferred_element_type=F32)
        o_ref[rows, :] = (o * (1.0 / denom)).astype(o_ref.dtype)


def _mla_body(q_ref, knc_ref, knx_ref, krc_ref, krx_ref, vc_ref, vx_ref, o_ref, *, scale):
    kc = jnp.concatenate([knc_ref[...], krc_ref[...]], axis=1)
    kx = jnp.concatenate([knx_ref[...], krx_ref[...]], axis=1)
    _attend(q_ref, kc, kx, vc_ref[...], vx_ref[...], o_ref, scale)


def mla_attention(q, kv, kr, n_batch, seq, ctx_len, tq=1024):
    nq = seq // tq
    cb0 = n_batch * seq // ctx_len
    scale = 1.0 / math.sqrt(MLA_NOPE + MLA_ROPE)
    return pl.pallas_call(
        functools.partial(_mla_body, scale=scale),
        grid=(n_batch, MLA_HEADS, nq),
        in_specs=[
            pl.BlockSpec((tq, 2 * LANE), lambda b, h, i: (b * nq + i, h)),
            pl.BlockSpec((ctx_len, LANE), lambda b, h, i: (cb0 + b, 2 * h)),
            pl.BlockSpec((seq, LANE), lambda b, h, i: (b, 2 * h)),
            pl.BlockSpec((ctx_len, LANE), lambda b, h, i: (cb0 + b, 0)),
            pl.BlockSpec((seq, LANE), lambda b, h, i: (b, 0)),
            pl.BlockSpec((ctx_len, LANE), lambda b, h, i: (cb0 + b, 2 * h + 1)),
            pl.BlockSpec((seq, LANE), lambda b, h, i: (b, 2 * h + 1)),
        ],
        out_specs=pl.BlockSpec((tq, LANE), lambda b, h, i: (b * nq + i, h)),
        out_shape=jax.ShapeDtypeStruct((n_batch * seq, MLA_HEADS * MLA_V), BF16),
        compiler_params=_params(3),
        name="mla_attention",
    )(q, kv, kv, kr, kr, kv, kv)


def _gqa_body(q_ref, kc_ref, kx_ref, vc_ref, vx_ref, o_ref, *, scale):
    _attend(q_ref, kc_ref[...], kx_ref[...], vc_ref[...], vx_ref[...], o_ref, scale)


def gqa_attention(q, k, v, n_batch, seq, ctx_len, tq=1024):
    nq = seq // tq
    cb0 = n_batch * seq // ctx_len
    rep = GQA_HEADS // GQA_KV_HEADS
    scale = 1.0 / math.sqrt(GQA_HEAD_DIM)
    return pl.pallas_call(
        functools.partial(_gqa_body, scale=scale),
        grid=(n_batch, GQA_HEADS, nq),
        in_specs=[
            pl.BlockSpec((tq, LANE), lambda b, h, i: (b * nq + i, h)),
            pl.BlockSpec((ctx_len, LANE), lambda b, h, i: (cb0 + b, h // rep)),
            pl.BlockSpec((seq, LANE), lambda b, h, i: (b, h // rep)),
            pl.BlockSpec((ctx_len, LANE), lambda b, h, i: (cb0 + b, h // rep)),
            pl.BlockSpec((seq, LANE), lambda b, h, i: (b, h // rep)),
        ],
        out_specs=pl.BlockSpec((tq, LANE), lambda b, h, i: (b * nq + i, h)),
        out_shape=jax.ShapeDtypeStruct((n_batch * seq, GQA_HEADS * GQA_HEAD_DIM), BF16),
        compiler_params=_params(3),
        name="gqa_attention",
    )(q, k, k, v, v)


def _rope_tables(n_batch, seq, ctx_len):
    rows = seq // GRID_W
    row = jnp.repeat(jnp.arange(rows, dtype=F32), GRID_W)
    col = jnp.tile(jnp.arange(GRID_W, dtype=F32), rows)

    def cos_sin(rot_dim):
        n_freq = rot_dim // 4
        inv = ROPE_THETA ** (-jnp.arange(n_freq, dtype=F32) / n_freq)
        ang = jnp.concatenate([row[:, None] * inv, col[:, None] * inv], axis=-1)
        return jnp.cos(ang), jnp.sin(ang)

    def over_tokens(x_part, ctx_value):
        x_rows = jnp.tile(x_part, (n_batch, 1))
        c_rows = jnp.broadcast_to(ctx_value, (n_batch * ctx_len, x_part.shape[1]))
        return jnp.concatenate([x_rows, c_rows], axis=0)

    cg, sg = cos_sin(GQA_HEAD_DIM)
    cos_g = over_tokens(jnp.concatenate([cg, cg], axis=1), jnp.ones((1, LANE), F32))
    sin_g = over_tokens(jnp.concatenate([-sg, sg], axis=1), jnp.zeros((1, LANE), F32))
    cm, sm = cos_sin(MLA_ROPE)
    zpad = jnp.zeros((seq, LANE - MLA_ROPE), F32)
    ident = jnp.concatenate([jnp.ones((1, MLA_ROPE), F32), jnp.zeros((1, LANE - MLA_ROPE), F32)], axis=1)
    cos_m = over_tokens(jnp.concatenate([cm, cm, zpad], axis=1), ident)
    sin_m = over_tokens(jnp.concatenate([-sm, sm, zpad], axis=1), jnp.zeros((1, LANE), F32))
    sin_m_abs = over_tokens(jnp.concatenate([sm, sm, zpad], axis=1), jnp.zeros((1, LANE), F32))
    return cos_g, sin_g, cos_m, sin_m, sin_m_abs


def _mla_q_weight(w_q_b):
    r = w_q_b.shape[0]
    w = w_q_b.reshape(r, MLA_HEADS, MLA_NOPE + MLA_ROPE)
    nope, rope = w[..., :MLA_NOPE], w[..., MLA_NOPE:]
    x1, x2 = rope[..., : MLA_ROPE // 2], rope[..., MLA_ROPE // 2 :]
    return jnp.concatenate([nope, rope, -x2, x1], axis=-1).reshape(r, MLA_HEADS * 2 * LANE).astype(BF16)


def _odd_kv_weight(w_in, q_cols):
    w = w_in[:, q_cols:]
    pad = jnp.zeros((w.shape[0], LANE - MLA_ROPE), w.dtype)
    split = MLA_KV_RANK + MLA_ROPE
    return jnp.concatenate([w[:, :split], pad, w[:, split:]], axis=1).astype(BF16)[None]


def kernel(x, c, ctx, c_ctx, mod_w, mod_b, norm1_g, norm2_g, final_g, ffn_w_gu, ffn_w_down, ev_w_in, ev_pool_w,
           ev_pool_scale, ev_sgu_norm_g, ev_sgu_w_s, ev_sgu_b, ev_w_out, od_w_in, od_q_a_g, od_w_q_b, od_kv_a_g,
           od_w_kv_b, od_q_norm_g, od_k_norm_g, od_w_out):
    B, S, D = x.shape
    CL = ctx.shape[1]
    depth = mod_w.shape[0]
    assert depth == 2, "layer schedule below is written for one even and one odd layer"
    n_x = B * S
    n_all = n_x + B * CL
    x_rows = x.reshape(n_x, D)
    c_rows = ctx.reshape(B * CL, D)

    cc = jnp.concatenate([c, c_ctx[None, :], jnp.zeros((MOD_ROWS - B - 1, D), F32)], axis=0)
    m = adaln_all(cc, mod_w, mod_b)
    table = m.reshape(depth, MOD_ROWS, 6, D).transpose(0, 2, 1, 3).reshape(depth * 6 * MOD_ROWS, 1, D)
    mods = Mods(table, B, S)
    w_gu = ffn_w_gu.astype(BF16)
    w_down = ffn_w_down.astype(BF16)

    def ffn(t_in, layer, m_rows):
        h = norm_mod([t_in], norm2_g[layer], mods, layer, SH2, SC2)
        act = matmul_swiglu(h, w_gu, layer, m_rows, tm=2048 if m_rows % 2048 == 0 else 1024)
        return matmul_residual([act], w_down, layer, [t_in], mods, layer, G2, m_rows, tm=512, tn=512,
                               name="ffn_down")

    pool_width = ev_pool_scale.shape[1]
    h = norm_mod([x_rows, c_rows], norm1_g[0], mods, 0, SH1, SC1)
    z = matmul(h, ev_w_in.astype(BF16), 0, n_all, ev_w_in.shape[2], F32, name="even_in")
    ya = pool_mixer(z, ev_pool_w[0], ev_pool_scale[0], B, S, CL)
    yb = sgu_mixer(z, ev_sgu_norm_g[0], ev_sgu_w_s[0], ev_sgu_b[0], pool_width)
    t = matmul_residual([ya, yb], ev_w_out.astype(BF16), 0, [x_rows, c_rows], mods, 0, G1, n_all,
                        name="even_out")
    t = ffn(t, 0, n_all)

    q_cols = MLA_Q_RANK + GQA_HEADS * GQA_HEAD_DIM
    cos_g, sin_g, cos_m, sin_m, sin_m_abs = _rope_tables(B, S, CL)
    h = norm_mod([t], norm1_g[1], mods, 1, SH1, SC1)
    zq = matmul(h, od_w_in.astype(BF16), 0, n_x, q_cols, F32, name="odd_in_q")
    w_kv = _odd_kv_weight(od_w_in[0], q_cols)
    zkv = matmul(h, w_kv, 0, n_all, w_kv.shape[2], F32, tm=512, tn=w_kv.shape[2], name="odd_in_kv")
    cq, gq = prep_q(zq, od_q_a_g[0], od_q_norm_g[0], cos_g, sin_g)
    ckv, kr, gk, gv = prep_kv(zkv, od_kv_a_g[0], od_k_norm_g[0], cos_g, sin_g, cos_m, sin_m)
    q_m = matmul_qrope(cq, _mla_q_weight(od_w_q_b[0]), cos_m, sin_m_abs)
    kv_m = matmul(ckv, od_w_kv_b.astype(BF16), 0, n_all, od_w_kv_b.shape[2], BF16, tn=1024, name="mla_kv_up")
    o_m = mla_attention(q_m, kv_m, kr, B, S, CL)
    o_g = gqa_attention(gq, gk, gv, B, S, CL)
    t = matmul_residual([o_m, o_g], od_w_out.astype(BF16), 0, [t], mods, 1, G1, n_x, name="odd_out")
    t = ffn(t, 1, n_x)

    return final_norm(t, final_g, n_x).reshape(B, S, D)
```

```python
import functools
import math

import jax
import jax.numpy as jnp
from jax import lax
from jax.experimental import pallas as pl
from jax.experimental.pallas import tpu as pltpu

F32 = jnp.float32
BF16 = jnp.bfloat16

GRID_W = 64
EPS = 1e-6
ROPE_THETA = 10000.0
POOL_WINDOWS = (2, 4, 8, 16)
POOL_HALO = 16
SGU_CHUNK = 128
MLA_HEADS = 16
MLA_Q_RANK = 1024
MLA_KV_RANK = 512
MLA_NOPE = 128
MLA_ROPE = 64
MLA_V = 128
GQA_HEADS = 16
GQA_KV_HEADS = 4
GQA_HEAD_DIM = 128
LANE = 128
MOD_ROWS = 16
SH1, SC1, G1, SH2, SC2, G2 = range(6)

VMEM_LIMIT = 56 * 1024 * 1024


def _params(n_axes):
    return pltpu.CompilerParams(dimension_semantics=("arbitrary",) * n_axes, vmem_limit_bytes=VMEM_LIMIT)


def _silu(x):
    return x / (1.0 + jnp.exp(-x))


def _gelu_tanh(x):
    return 0.5 * x * (1.0 + jnp.tanh(math.sqrt(2.0 / math.pi) * (x + 0.044715 * (x * x * x))))


def _rms(x, g):
    return x * lax.rsqrt(jnp.mean(x * x, axis=-1, keepdims=True) + EPS) * g


def _row_sources(srcs, tm, width, col_of):
    n0 = srcs[0].shape[0] // tm
    if len(srcs) == 1:
        return n0, [pl.BlockSpec((tm, width), lambda i, *r: (i, col_of(*r)))]
    n1 = srcs[1].shape[0] // tm
    return n0, [
        pl.BlockSpec((tm, width), lambda i, *r: (jnp.minimum(i, n0 - 1), col_of(*r))),
        pl.BlockSpec((tm, width), lambda i, *r: (jnp.clip(i - n0, 0, n1 - 1), col_of(*r))),
    ]


def _from_source(refs, n0, i):
    if len(refs) == 1:
        return refs[0][...]
    return jnp.where(i < n0, refs[0][...], refs[1][...])


def _adaln_body(c_ref, w_ref, b_ref, o_ref):
    sc = _silu(c_ref[...]).astype(BF16)
    o_ref[0] = jnp.dot(sc, w_ref[0].astype(BF16), preferred_element_type=F32) + b_ref[0]


def adaln_all(cc, mod_w, mod_b, tn=1024):
    L, D, N = mod_w.shape
    return pl.pallas_call(
        _adaln_body,
        grid=(L, N // tn),
        in_specs=[
            pl.BlockSpec((MOD_ROWS, D), lambda l, j: (0, 0)),
            pl.BlockSpec((1, D, tn), lambda l, j: (l, 0, j)),
            pl.BlockSpec((1, 1, tn), lambda l, j: (l, 0, j)),
        ],
        out_specs=pl.BlockSpec((1, MOD_ROWS, tn), lambda l, j: (l, 0, j)),
        out_shape=jax.ShapeDtypeStruct((L, MOD_ROWS, N), F32),
        compiler_params=_params(2),
        name="adaln",
    )(cc, mod_w, mod_b.reshape(L, 1, N))


class Mods:
    def __init__(self, table, n_batch, seq):
        self.table = table
        self.n_batch = n_batch
        self.seq = seq

    def index(self, layer, which, tm):
        n_x = self.n_batch * self.seq // tm
        base = (layer * 6 + which) * MOD_ROWS
        return lambda i: base + jnp.where(i < n_x, (i * tm) // self.seq, self.n_batch)


def _norm_mod_body(*refs, n_src, n0):
    g_ref, sh_ref, sc_ref, o_ref = refs[n_src:]

    def emit(x_ref):
        y = _rms(x_ref[...], g_ref[...])
        o_ref[...] = (y * (1.0 + sc_ref[0]) + sh_ref[0]).astype(o_ref.dtype)

    if n_src == 1:
        emit(refs[0])
    else:
        i = pl.program_id(0)
        pl.when(i < n0)(lambda: emit(refs[0]))
        pl.when(i >= n0)(lambda: emit(refs[1]))


def norm_mod(srcs, gain, mods, layer, which_shift, which_scale, tm=512):
    D = srcs[0].shape[1]
    M = sum(s.shape[0] for s in srcs)
    sh_idx = mods.index(layer, which_shift, tm)
    sc_idx = mods.index(layer, which_scale, tm)
    n0, src_specs = _row_sources(srcs, tm, D, lambda: 0)
    return pl.pallas_call(
        functools.partial(_norm_mod_body, n_src=len(srcs), n0=n0),
        grid=(M // tm,),
        in_specs=src_specs
        + [
            pl.BlockSpec((1, D), lambda i: (0, 0)),
            pl.BlockSpec((1, 1, D), lambda i: (sh_idx(i), 0, 0)),
            pl.BlockSpec((1, 1, D), lambda i: (sc_idx(i), 0, 0)),
        ],
        out_specs=pl.BlockSpec((tm, D), lambda i: (i, 0)),
        out_shape=jax.ShapeDtypeStruct((M, D), BF16),
        compiler_params=_params(1),
        name="norm_mod",
    )(*srcs, gain.reshape(1, D), mods.table, mods.table)


def _rms_body(x_ref, g_ref, o_ref):
    o_ref[...] = _rms(x_ref[...], g_ref[...]).astype(o_ref.dtype)


def final_norm(t, gain, m_rows, tm=512):
    D = t.shape[1]
    return pl.pallas_call(
        _rms_body,
        grid=(m_rows // tm,),
        in_specs=[pl.BlockSpec((tm, D), lambda i: (i, 0)), pl.BlockSpec((1, D), lambda i: (0, 0))],
        out_specs=pl.BlockSpec((tm, D), lambda i: (i, 0)),
        out_shape=jax.ShapeDtypeStruct((m_rows, D), F32),
        compiler_params=_params(1),
        name="final_norm",
    )(t, gain.reshape(1, D))


def _mm_body(a_ref, w_ref, o_ref):
    o_ref[...] = jnp.dot(a_ref[...], w_ref[...].astype(BF16), preferred_element_type=F32).astype(o_ref.dtype)


def matmul(a, w, wl, m_rows, n_cols, out_dtype, tm=1024, tn=512, name="mm"):
    K = w.shape[1]
    return pl.pallas_call(
        _mm_body,
        grid=(m_rows // tm, n_cols // tn),
        in_specs=[
            pl.BlockSpec((tm, K), lambda i, j: (i, 0)),
            pl.BlockSpec((None, K, tn), lambda i, j: (wl, 0, j)),
        ],
        out_specs=pl.BlockSpec((tm, tn), lambda i, j: (i, j)),
        out_shape=jax.ShapeDtypeStruct((m_rows, n_cols), out_dtype),
        compiler_params=_params(2),
        name=name,
    )(a, w)


def _mm_res_body(*refs, n_a, n_res, n0):
    a_refs, w_refs = refs[:n_a], refs[n_a : 2 * n_a]
    r_refs = refs[2 * n_a : 2 * n_a + n_res]
    g_ref, o_ref = refs[2 * n_a + n_res :]
    acc = jnp.dot(a_refs[0][...], w_refs[0][...].astype(BF16), preferred_element_type=F32)
    for a_ref, w_ref in zip(a_refs[1:], w_refs[1:]):
        acc = acc + jnp.dot(a_ref[...], w_ref[...].astype(BF16), preferred_element_type=F32)
    o_ref[...] = _from_source(r_refs, n0, pl.program_id(0)) + g_ref[0] * acc


def matmul_residual(a_parts, w, wl, res_srcs, mods, layer, which_gate, m_rows, tm=1024, tn=512, name="mm_res"):
    K, N = w.shape[1:]
    kp = K // len(a_parts)
    g_idx = mods.index(layer, which_gate, tm)
    n0, res_specs = _row_sources(res_srcs, tm, tn, lambda j: j)
    a_specs = [pl.BlockSpec((tm, kp), lambda i, j: (i, 0)) for _ in a_parts]
    w_specs = [pl.BlockSpec((None, kp, tn), lambda i, j, p=p: (wl, p, j)) for p in range(len(a_parts))]
    return pl.pallas_call(
        functools.partial(_mm_res_body, n_a=len(a_parts), n_res=len(res_srcs), n0=n0),
        grid=(m_rows // tm, N // tn),
        in_specs=a_specs + w_specs + res_specs + [pl.BlockSpec((1, 1, tn), lambda i, j: (g_idx(i), 0, j))],
        out_specs=pl.BlockSpec((tm, tn), lambda i, j: (i, j)),
        out_shape=jax.ShapeDtypeStruct((m_rows, N), F32),
        compiler_params=_params(2),
        name=name,
    )(*a_parts, *([w] * len(a_parts)), *res_srcs, mods.table)


def _swiglu_body(a_ref, wg_ref, wu_ref, o_ref):
    a = a_ref[...]
    g = jnp.dot(a, wg_ref[...].astype(BF16), preferred_element_type=F32)
    u = jnp.dot(a, wu_ref[...].astype(BF16), preferred_element_type=F32)
    o_ref[...] = (_silu(g) * u).astype(o_ref.dtype)


def matmul_swiglu(a, w_gu, wl, m_rows, tm=2048, tn=256):
    K, N2 = w_gu.shape[1:]
    H = N2 // 2
    nj = H // tn
    return pl.pallas_call(
        _swiglu_body,
        grid=(m_rows // tm, nj),
        in_specs=[
            pl.BlockSpec((tm, K), lambda i, j: (i, 0)),
            pl.BlockSpec((None, K, tn), lambda i, j: (wl, 0, j)),
            pl.BlockSpec((None, K, tn), lambda i, j: (wl, 0, j + nj)),
        ],
        out_specs=pl.BlockSpec((tm, tn), lambda i, j: (i, j)),
        out_shape=jax.ShapeDtypeStruct((m_rows, H), BF16),
        compiler_params=_params(2),
        name="ffn_up",
    )(a, w_gu, w_gu)


def _pool_body(z_ref, w_ref, s_ref, o_ref, *, n_batch, ctx_len):
    b = pl.program_id(0)
    g = pl.program_id(1)
    rows, C = z_ref.shape

    def run(sub_len):
        n_sub = rows // sub_len
        stride = sub_len + POOL_HALO
        z = z_ref[...]
        gap = jnp.zeros((POOL_HALO, C), F32)
        zp = jnp.concatenate([p for k in range(n_sub) for p in (z[k * sub_len : (k + 1) * sub_len], gap)], axis=0)
        total = n_sub * stride
        t1 = lax.broadcasted_iota(jnp.int32, (sub_len, C), 0)
        t = t1 if n_sub == 1 else jnp.concatenate([t1] * n_sub, axis=0)

        def prev(a, k):
            return pltpu.roll(a, k, 0)

        def nxt(a, k):
            return pltpu.roll(a, total - k, 0)

        def window_sum(level):
            s = zp + prev(zp, 1)
            for lv in range(level):
                s = prev(s, 2**lv) + nxt(s, 2**lv)
            if n_sub == 1:
                return s[:sub_len]
            return jnp.concatenate([s[k * stride : k * stride + sub_len] for k in range(n_sub)], axis=0)

        for gi, window in enumerate(POOL_WINDOWS):

            @pl.when(g == gi)
            def _(gi=gi, window=window):
                half = window // 2
                cnt = (jnp.minimum(t - half + window, sub_len) - jnp.maximum(t - half, 0)).astype(F32)
                pooled = window_sum(gi) / cnt - z
                mixed = jnp.dot(pooled.astype(BF16), w_ref[0].astype(BF16), preferred_element_type=F32)
                o_ref[...] = (mixed * s_ref[...]).astype(o_ref.dtype)

    pl.when(b < n_batch)(lambda: run(rows))
    pl.when(b >= n_batch)(lambda: run(ctx_len))


def pool_mixer(z, pool_w, pool_scale, n_batch, seq, ctx_len):
    M = z.shape[0]
    n_groups, G, _ = pool_w.shape
    assert (M - n_batch * seq) % seq == 0 and seq % ctx_len == 0
    return pl.pallas_call(
        functools.partial(_pool_body, n_batch=n_batch, ctx_len=ctx_len),
        grid=(M // seq, n_groups),
        in_specs=[
            pl.BlockSpec((seq, G), lambda b, g: (b, g)),
            pl.BlockSpec((1, G, G), lambda b, g: (g, 0, 0)),
            pl.BlockSpec((1, G), lambda b, g: (0, g)),
        ],
        out_specs=pl.BlockSpec((seq, G), lambda b, g: (b, g)),
        out_shape=jax.ShapeDtypeStruct((M, n_groups * G), BF16),
        compiler_params=_params(2),
        name="pool_mixer",
    )(z, pool_w, pool_scale.reshape(1, n_groups * G))


def _sgu_body(u_ref, v_ref, ng_ref, ws_ref, bs_ref, o_ref):
    ws = ws_ref[0].astype(BF16)
    bs = bs_ref[0]
    ng = ng_ref[0]
    for c in range(u_ref.shape[0] // SGU_CHUNK):
        rows = pl.ds(c * SGU_CHUNK, SGU_CHUNK)
        vn = _rms(_gelu_tanh(v_ref[rows, :]), ng)
        gate = jnp.dot(ws, vn.astype(BF16), preferred_element_type=F32) + bs
        o_ref[rows, :] = (_gelu_tanh(u_ref[rows, :]) * gate).astype(o_ref.dtype)


def sgu_mixer(z, norm_g, w_s, b_s, col0, ts=1024):
    M = z.shape[0]
    H, P, _ = w_s.shape
    hd = norm_g.shape[1]
    cb0 = col0 // hd
    return pl.pallas_call(
        _sgu_body,
        grid=(M // ts, H),
        in_specs=[
            pl.BlockSpec((ts, hd), lambda i, h: (i, cb0 + h)),
            pl.BlockSpec((ts, hd), lambda i, h: (i, cb0 + H + h)),
            pl.BlockSpec((1, 1, hd), lambda i, h: (h, 0, 0)),
            pl.BlockSpec((1, P, P), lambda i, h: (h, 0, 0)),
            pl.BlockSpec((1, P, 1), lambda i, h: (h, 0, 0)),
        ],
        out_specs=pl.BlockSpec((ts, hd), lambda i, h: (i, h)),
        out_shape=jax.ShapeDtypeStruct((M, H * hd), BF16),
        compiler_params=_params(2),
        name="sgu_mixer",
    )(z, z, norm_g.reshape(H, 1, hd), w_s, b_s.reshape(H, P, 1))


def _rope_half(x, cos, sin_signed):
    return x * cos + pltpu.roll(x, LANE // 2, 1) * sin_signed


def _prep_q_body(zq_ref, qag_ref, qng_ref, cos_ref, sin_ref, cq_ref, gq_ref):
    cq_ref[...] = _rms(zq_ref[:, :MLA_Q_RANK], qag_ref[...]).astype(cq_ref.dtype)
    qs = _exp2_scale(1.0 / math.sqrt(GQA_HEAD_DIM))
    cos = cos_ref[...] * qs
    sin = sin_ref[...] * qs
    g = qng_ref[...]
    for h in range(GQA_HEADS):
        x = _rms(zq_ref[:, MLA_Q_RANK + h * LANE : MLA_Q_RANK + (h + 1) * LANE], g)
        gq_ref[:, h * LANE : (h + 1) * LANE] = _rope_half(x, cos, sin).astype(gq_ref.dtype)


def prep_q(zq, q_a_g, q_norm_g, cos_g, sin_g, tm=512):
    M, N = zq.shape
    nq = GQA_HEADS * GQA_HEAD_DIM
    return pl.pallas_call(
        _prep_q_body,
        grid=(M // tm,),
        in_specs=[
            pl.BlockSpec((tm, N), lambda i: (i, 0)),
            pl.BlockSpec((1, MLA_Q_RANK), lambda i: (0, 0)),
            pl.BlockSpec((1, LANE), lambda i: (0, 0)),
            pl.BlockSpec((tm, LANE), lambda i: (i, 0)),
            pl.BlockSpec((tm, LANE), lambda i: (i, 0)),
        ],
        out_specs=[pl.BlockSpec((tm, MLA_Q_RANK), lambda i: (i, 0)), pl.BlockSpec((tm, nq), lambda i: (i, 0))],
        out_shape=[jax.ShapeDtypeStruct((M, MLA_Q_RANK), BF16), jax.ShapeDtypeStruct((M, nq), BF16)],
        compiler_params=_params(1),
        name="prep_q",
    )(zq, q_a_g.reshape(1, -1), q_norm_g.reshape(1, -1), cos_g, sin_g)


def _prep_kv_body(z_ref, kag_ref, kng_ref, cg_ref, sg_ref, cm_ref, sm_ref, ckv_ref, kr_ref, gk_ref, gv_ref):
    ckv_ref[...] = _rms(z_ref[:, :MLA_KV_RANK], kag_ref[...]).astype(ckv_ref.dtype)
    kr = z_ref[:, MLA_KV_RANK : MLA_KV_RANK + LANE]
    lane = lax.broadcasted_iota(jnp.int32, kr.shape, 1)
    half = MLA_ROPE // 2
    swapped = jnp.where(lane < half, pltpu.roll(kr, LANE - half, 1), pltpu.roll(kr, half, 1))
    kr_ref[...] = (kr * cm_ref[...] + swapped * sm_ref[...]).astype(kr_ref.dtype)
    off_k = MLA_KV_RANK + LANE
    off_v = off_k + GQA_KV_HEADS * LANE
    cos = cg_ref[...]
    sin = sg_ref[...]
    g = kng_ref[...]
    for h in range(GQA_KV_HEADS):
        x = _rms(z_ref[:, off_k + h * LANE : off_k + (h + 1) * LANE], g)
        gk_ref[:, h * LANE : (h + 1) * LANE] = _rope_half(x, cos, sin).astype(gk_ref.dtype)
    gv_ref[...] = z_ref[:, off_v : off_v + GQA_KV_HEADS * LANE].astype(gv_ref.dtype)


def prep_kv(zkv, kv_a_g, k_norm_g, cos_g, sin_g, cos_m, sin_m, tm=512):
    M, N = zkv.shape
    nk = GQA_KV_HEADS * GQA_HEAD_DIM
    tok = lambda i: (i, 0)
    fixed = lambda i: (0, 0)
    return pl.pallas_call(
        _prep_kv_body,
        grid=(M // tm,),
        in_specs=[
            pl.BlockSpec((tm, N), tok),
            pl.BlockSpec((1, MLA_KV_RANK), fixed),
            pl.BlockSpec((1, LANE), fixed),
            pl.BlockSpec((tm, LANE), tok),
            pl.BlockSpec((tm, LANE), tok),
            pl.BlockSpec((tm, LANE), tok),
            pl.BlockSpec((tm, LANE), tok),
        ],
        out_specs=[
            pl.BlockSpec((tm, MLA_KV_RANK), tok),
            pl.BlockSpec((tm, LANE), tok),
            pl.BlockSpec((tm, nk), tok),
            pl.BlockSpec((tm, nk), tok),
        ],
        out_shape=[
            jax.ShapeDtypeStruct((M, MLA_KV_RANK), BF16),
            jax.ShapeDtypeStruct((M, LANE), BF16),
            jax.ShapeDtypeStruct((M, nk), BF16),
            jax.ShapeDtypeStruct((M, nk), BF16),
        ],
        compiler_params=_params(1),
        name="prep_kv",
    )(zkv, kv_a_g.reshape(1, -1), k_norm_g.reshape(1, -1), cos_g, sin_g, cos_m, sin_m)


def _mm_qrope_body(a_ref, w_ref, c_ref, s_ref, o_ref):
    acc = jnp.dot(a_ref[...], w_ref[...], preferred_element_type=F32) * _exp2_scale(
        1.0 / math.sqrt(MLA_NOPE + MLA_ROPE)
    )
    cos = c_ref[...]
    sin = s_ref[...]
    for h in range(acc.shape[1] // (2 * LANE)):
        lo = h * 2 * LANE
        o_ref[:, lo : lo + LANE] = acc[:, lo : lo + LANE].astype(o_ref.dtype)
        t = acc[:, lo + LANE : lo + 2 * LANE]
        o_ref[:, lo + LANE : lo + 2 * LANE] = (t * cos + pltpu.roll(t, LANE // 2, 1) * sin).astype(o_ref.dtype)


def matmul_qrope(a, w, cos_m, sin_m_abs, tm=1024, tn=1024):
    M, K = a.shape
    N = w.shape[1]
    return pl.pallas_call(
        _mm_qrope_body,
        grid=(M // tm, N // tn),
        in_specs=[
            pl.BlockSpec((tm, K), lambda i, j: (i, 0)),
            pl.BlockSpec((K, tn), lambda i, j: (0, j)),
            pl.BlockSpec((tm, LANE), lambda i, j: (i, 0)),
            pl.BlockSpec((tm, LANE), lambda i, j: (i, 0)),
        ],
        out_specs=pl.BlockSpec((tm, tn), lambda i, j: (i, j)),
        out_shape=jax.ShapeDtypeStruct((M, N), BF16),
        compiler_params=_params(2),
        name="mla_q_up",
    )(a, w, cos_m, sin_m_abs)


_NT = (((1,), (1,)), ((), ()))
ATTN_SUB = 256


def _exp2_scale(scale):
    return scale * math.log2(math.e)


def _attend(q_ref, kc, kx, vc, vx, o_ref, s_ref):
    n_c = kc.shape[0]
    n_sub = q_ref.shape[0] // ATTN_SUB

    def scores(r):
        slot = r % s_ref.shape[0]
        q = q_ref[pl.ds(r * ATTN_SUB, ATTN_SUB), :]
        s_ref[slot, :, :n_c] = lax.dot_general(q, kc, _NT, preferred_element_type=F32)
        s_ref[slot, :, n_c:] = lax.dot_general(q, kx, _NT, preferred_element_type=F32)
        s = s_ref[slot]
        return s, jnp.max(s, axis=-1, keepdims=True)

    nxt = scores(0)
    for r in range(n_sub):
        s, m = nxt
        p = jnp.exp2(s - m)
        denom = jnp.sum(p, axis=-1, keepdims=True)
        p_c = p[:, :n_c]
        if r + 1 < n_sub:
            nxt = scores(r + 1)
            p_c = p_c + _zero_after(nxt[1])
        o = jnp.dot(p_c.astype(BF16), vc, preferred_element_type=F32)
        o = o + jnp.dot(p[:, n_c:].astype(BF16), vx, preferred_element_type=F32)
        o_ref[pl.ds(r * ATTN_SUB, ATTN_SUB), :] = (o * (1.0 / denom)).astype(o_ref.dtype)


def _zero_after(x):
    bits = lax.bitcast_convert_type(x, jnp.uint32)
    return lax.bitcast_convert_type((bits >> 16) >> 16, F32)


def _score_scratch(n_keys):
    return [pltpu.VMEM((2, ATTN_SUB, n_keys), F32)]


def _mla_body(q_ref, knc_ref, knx_ref, krc_ref, krx_ref, vc_ref, vx_ref, o_ref, s_ref):
    kc = jnp.concatenate([knc_ref[...], krc_ref[...]], axis=1)
    kx = jnp.concatenate([knx_ref[...], krx_ref[...]], axis=1)
    _attend(q_ref, kc, kx, vc_ref[...], vx_ref[...], o_ref, s_ref)


def mla_attention(q, kv, kr, n_batch, seq, ctx_len, tq=2048):
    nq = seq // tq
    cb0 = n_batch * seq // ctx_len
    return pl.pallas_call(
        _mla_body,
        grid=(n_batch, MLA_HEADS, nq),
        in_specs=[
            pl.BlockSpec((tq, 2 * LANE), lambda b, h, i: (b * nq + i, h)),
            pl.BlockSpec((ctx_len, LANE), lambda b, h, i: (cb0 + b, 2 * h)),
            pl.BlockSpec((seq, LANE), lambda b, h, i: (b, 2 * h)),
            pl.BlockSpec((ctx_len, LANE), lambda b, h, i: (cb0 + b, 0)),
            pl.BlockSpec((seq, LANE), lambda b, h, i: (b, 0)),
            pl.BlockSpec((ctx_len, LANE), lambda b, h, i: (cb0 + b, 2 * h + 1)),
            pl.BlockSpec((seq, LANE), lambda b, h, i: (b, 2 * h + 1)),
        ],
        out_specs=pl.BlockSpec((tq, LANE), lambda b, h, i: (b * nq + i, h)),
        out_shape=jax.ShapeDtypeStruct((n_batch * seq, MLA_HEADS * MLA_V), BF16),
        scratch_shapes=_score_scratch(ctx_len + seq),
        compiler_params=_params(3),
        name="mla_attention",
    )(q, kv, kv, kr, kr, kv, kv)


def _gqa_body(q_ref, kc_ref, kx_ref, vc_ref, vx_ref, o_ref, s_ref):
    _attend(q_ref, kc_ref[...], kx_ref[...], vc_ref[...], vx_ref[...], o_ref, s_ref)


def gqa_attention(q, k, v, n_batch, seq, ctx_len, tq=2048):
    nq = seq // tq
    cb0 = n_batch * seq // ctx_len
    rep = GQA_HEADS // GQA_KV_HEADS
    return pl.pallas_call(
        _gqa_body,
        grid=(n_batch, GQA_HEADS, nq),
        in_specs=[
            pl.BlockSpec((tq, LANE), lambda b, h, i: (b * nq + i, h)),
            pl.BlockSpec((ctx_len, LANE), lambda b, h, i: (cb0 + b, h // rep)),
            pl.BlockSpec((seq, LANE), lambda b, h, i: (b, h // rep)),
            pl.BlockSpec((ctx_len, LANE), lambda b, h, i: (cb0 + b, h // rep)),
            pl.BlockSpec((seq, LANE), lambda b, h, i: (b, h // rep)),
        ],
        out_specs=pl.BlockSpec((tq, LANE), lambda b, h, i: (b * nq + i, h)),
        out_shape=jax.ShapeDtypeStruct((n_batch * seq, GQA_HEADS * GQA_HEAD_DIM), BF16),
        scratch_shapes=_score_scratch(ctx_len + seq),
        compiler_params=_params(3),
        name="gqa_attention",
    )(q, k, k, v, v)


def _rope_tables(n_batch, seq, ctx_len):
    rows = seq // GRID_W
    row = jnp.repeat(jnp.arange(rows, dtype=F32), GRID_W)
    col = jnp.tile(jnp.arange(GRID_W, dtype=F32), rows)

    def cos_sin(rot_dim):
        n_freq = rot_dim // 4
        inv = ROPE_THETA ** (-jnp.arange(n_freq, dtype=F32) / n_freq)
        ang = jnp.concatenate([row[:, None] * inv, col[:, None] * inv], axis=-1)
        return jnp.cos(ang), jnp.sin(ang)

    def over_tokens(x_part, ctx_value):
        x_rows = jnp.tile(x_part, (n_batch, 1))
        c_rows = jnp.broadcast_to(ctx_value, (n_batch * ctx_len, x_part.shape[1]))
        return jnp.concatenate([x_rows, c_rows], axis=0)

    cg, sg = cos_sin(GQA_HEAD_DIM)
    cos_g = over_tokens(jnp.concatenate([cg, cg], axis=1), jnp.ones((1, LANE), F32))
    sin_g = over_tokens(jnp.concatenate([-sg, sg], axis=1), jnp.zeros((1, LANE), F32))
    cm, sm = cos_sin(MLA_ROPE)
    zpad = jnp.zeros((seq, LANE - MLA_ROPE), F32)
    ident = jnp.concatenate([jnp.ones((1, MLA_ROPE), F32), jnp.zeros((1, LANE - MLA_ROPE), F32)], axis=1)
    cos_m = over_tokens(jnp.concatenate([cm, cm, zpad], axis=1), ident)
    sin_m = over_tokens(jnp.concatenate([-sm, sm, zpad], axis=1), jnp.zeros((1, LANE), F32))
    sin_m_abs = over_tokens(jnp.concatenate([sm, sm, zpad], axis=1), jnp.zeros((1, LANE), F32))
    return cos_g, sin_g, cos_m, sin_m, sin_m_abs


def _mla_q_weight(w_q_b):
    r = w_q_b.shape[0]
    w = w_q_b.reshape(r, MLA_HEADS, MLA_NOPE + MLA_ROPE)
    nope, rope = w[..., :MLA_NOPE], w[..., MLA_NOPE:]
    x1, x2 = rope[..., : MLA_ROPE // 2], rope[..., MLA_ROPE // 2 :]
    return jnp.concatenate([nope, rope, -x2, x1], axis=-1).reshape(r, MLA_HEADS * 2 * LANE).astype(BF16)


def _odd_kv_weight(w_in, q_cols):
    w = w_in[:, q_cols:]
    pad = jnp.zeros((w.shape[0], LANE - MLA_ROPE), w.dtype)
    split = MLA_KV_RANK + MLA_ROPE
    return jnp.concatenate([w[:, :split], pad, w[:, split:]], axis=1).astype(BF16)[None]


def kernel(x, c, ctx, c_ctx, mod_w, mod_b, norm1_g, norm2_g, final_g, ffn_w_gu, ffn_w_down, ev_w_in, ev_pool_w,
           ev_pool_scale, ev_sgu_norm_g, ev_sgu_w_s, ev_sgu_b, ev_w_out, od_w_in, od_q_a_g, od_w_q_b, od_kv_a_g,
           od_w_kv_b, od_q_norm_g, od_k_norm_g, od_w_out):
    B, S, D = x.shape
    CL = ctx.shape[1]
    depth = mod_w.shape[0]
    assert depth == 2, "layer schedule below is written for one even and one odd layer"
    n_x = B * S
    n_all = n_x + B * CL
    x_rows = x.reshape(n_x, D)
    c_rows = ctx.reshape(B * CL, D)

    cc = jnp.concatenate([c, c_ctx[None, :], jnp.zeros((MOD_ROWS - B - 1, D), F32)], axis=0)
    m = adaln_all(cc, mod_w, mod_b)
    table = m.reshape(depth, MOD_ROWS, 6, D).transpose(0, 2, 1, 3).reshape(depth * 6 * MOD_ROWS, 1, D)
    mods = Mods(table, B, S)
    w_down = ffn_w_down.astype(BF16)

    def ffn(t_in, layer, m_rows):
        h = norm_mod([t_in], norm2_g[layer], mods, layer, SH2, SC2)
        act = matmul_swiglu(h, ffn_w_gu, layer, m_rows, tm=1024)
        return matmul_residual([act], w_down, layer, [t_in], mods, layer, G2, m_rows, tm=512, tn=512,
                               name="ffn_down")

    pool_width = ev_pool_scale.shape[1]
    h = norm_mod([x_rows, c_rows], norm1_g[0], mods, 0, SH1, SC1)
    z = matmul(h, ev_w_in, 0, n_all, ev_w_in.shape[2], F32, name="even_in")
    ya = pool_mixer(z, ev_pool_w[0], ev_pool_scale[0], B, S, CL)
    yb = sgu_mixer(z, ev_sgu_norm_g[0], ev_sgu_w_s[0], ev_sgu_b[0], pool_width)
    t = matmul_residual([ya, yb], ev_w_out, 0, [x_rows, c_rows], mods, 0, G1, n_all,
                        name="even_out")
    t = ffn(t, 0, n_all)

    q_cols = MLA_Q_RANK + GQA_HEADS * GQA_HEAD_DIM
    cos_g, sin_g, cos_m, sin_m, sin_m_abs = _rope_tables(B, S, CL)
    h = norm_mod([t], norm1_g[1], mods, 1, SH1, SC1)
    zq = matmul(h, od_w_in, 0, n_x, q_cols, F32, name="odd_in_q")
    w_kv = _odd_kv_weight(od_w_in[0], q_cols)
    zkv = matmul(h, w_kv, 0, n_all, w_kv.shape[2], F32, tm=512, tn=w_kv.shape[2], name="odd_in_kv")
    cq, gq = prep_q(zq, od_q_a_g[0], od_q_norm_g[0], cos_g, sin_g)
    ckv, kr, gk, gv = prep_kv(zkv, od_kv_a_g[0], od_k_norm_g[0], cos_g, sin_g, cos_m, sin_m)
    q_m = matmul_qrope(cq, _mla_q_weight(od_w_q_b[0]), cos_m, sin_m_abs)
    kv_m = matmul(ckv, od_w_kv_b, 0, n_all, od_w_kv_b.shape[2], BF16, tn=1024, name="mla_kv_up")
    o_m = mla_attention(q_m, kv_m, kr, B, S, CL)
    o_g = gqa_attention(gq, gk, gv, B, S, CL)
    t = matmul_residual([o_m, o_g], od_w_out, 0, [t], mods, 1, G1, n_x, name="odd_out")
    t = ffn(t, 1, n_x)

    return final_norm(t, final_g, n_x).reshape(B, S, D)
```

```python
import functools
import math

import jax
import jax.numpy as jnp
from jax import lax
from jax.experimental import pallas as pl
from jax.experimental.pallas import tpu as pltpu

F32 = jnp.float32
BF16 = jnp.bfloat16

GRID_W = 64
EPS = 1e-6
ROPE_THETA = 10000.0
POOL_WINDOWS = (2, 4, 8, 16)
POOL_HALO = 16
SGU_CHUNK = 128
MLA_HEADS = 16
MLA_Q_RANK = 1024
MLA_KV_RANK = 512
MLA_NOPE = 128
MLA_ROPE = 64
MLA_V = 128
GQA_HEADS = 16
GQA_KV_HEADS = 4
GQA_HEAD_DIM = 128
LANE = 128
MOD_ROWS = 16
SH1, SC1, G1, SH2, SC2, G2 = range(6)

VMEM_LIMIT = 56 * 1024 * 1024


def _params(n_axes):
    return pltpu.CompilerParams(dimension_semantics=("arbitrary",) * n_axes, vmem_limit_bytes=VMEM_LIMIT)


def _silu(x):
    return x / (1.0 + jnp.exp(-x))


def _gelu_tanh(x):
    return 0.5 * x * (1.0 + jnp.tanh(math.sqrt(2.0 / math.pi) * (x + 0.044715 * (x * x * x))))


def _rms(x, g):
    return x * lax.rsqrt(jnp.mean(x * x, axis=-1, keepdims=True) + EPS) * g


def _row_sources(srcs, tm, width, col_of):
    n0 = srcs[0].shape[0] // tm
    if len(srcs) == 1:
        return n0, [pl.BlockSpec((tm, width), lambda i, *r: (i, col_of(*r)))]
    n1 = srcs[1].shape[0] // tm
    return n0, [
        pl.BlockSpec((tm, width), lambda i, *r: (jnp.minimum(i, n0 - 1), col_of(*r))),
        pl.BlockSpec((tm, width), lambda i, *r: (jnp.clip(i - n0, 0, n1 - 1), col_of(*r))),
    ]


def _from_source(refs, n0, i):
    if len(refs) == 1:
        return refs[0][...]
    return jnp.where(i < n0, refs[0][...], refs[1][...])


def _adaln_body(c_ref, w_ref, b_ref, o_ref):
    sc = _silu(c_ref[...]).astype(BF16)
    o_ref[0] = jnp.dot(sc, w_ref[0].astype(BF16), preferred_element_type=F32) + b_ref[0]


def adaln_all(cc, mod_w, mod_b, tn=1024):
    L, D, N = mod_w.shape
    return pl.pallas_call(
        _adaln_body,
        grid=(L, N // tn),
        in_specs=[
            pl.BlockSpec((MOD_ROWS, D), lambda l, j: (0, 0)),
            pl.BlockSpec((1, D, tn), lambda l, j: (l, 0, j)),
            pl.BlockSpec((1, 1, tn), lambda l, j: (l, 0, j)),
        ],
        out_specs=pl.BlockSpec((1, MOD_ROWS, tn), lambda l, j: (l, 0, j)),
        out_shape=jax.ShapeDtypeStruct((L, MOD_ROWS, N), F32),
        compiler_params=_params(2),
        name="adaln",
    )(cc, mod_w, mod_b.reshape(L, 1, N))


class Mods:
    def __init__(self, table, n_batch, seq):
        self.table = table
        self.n_batch = n_batch
        self.seq = seq

    def index(self, layer, which, tm):
        n_x = self.n_batch * self.seq // tm
        base = (layer * 6 + which) * MOD_ROWS
        return lambda i: base + jnp.where(i < n_x, (i * tm) // self.seq, self.n_batch)


def _norm_mod_body(*refs, n_src, n0):
    g_ref, sh_ref, sc_ref, o_ref = refs[n_src:]

    def emit(x_ref):
        y = _rms(x_ref[...], g_ref[...])
        o_ref[...] = (y * (1.0 + sc_ref[0]) + sh_ref[0]).astype(o_ref.dtype)

    if n_src == 1:
        emit(refs[0])
    else:
        i = pl.program_id(0)
        pl.when(i < n0)(lambda: emit(refs[0]))
        pl.when(i >= n0)(lambda: emit(refs[1]))


def norm_mod(srcs, gain, mods, layer, which_shift, which_scale, tm=512):
    D = srcs[0].shape[1]
    M = sum(s.shape[0] for s in srcs)
    sh_idx = mods.index(layer, which_shift, tm)
    sc_idx = mods.index(layer, which_scale, tm)
    n0, src_specs = _row_sources(srcs, tm, D, lambda: 0)
    return pl.pallas_call(
        functools.partial(_norm_mod_body, n_src=len(srcs), n0=n0),
        grid=(M // tm,),
        in_specs=src_specs
        + [
            pl.BlockSpec((1, D), lambda i: (0, 0)),
            pl.BlockSpec((1, 1, D), lambda i: (sh_idx(i), 0, 0)),
            pl.BlockSpec((1, 1, D), lambda i: (sc_idx(i), 0, 0)),
        ],
        out_specs=pl.BlockSpec((tm, D), lambda i: (i, 0)),
        out_shape=jax.ShapeDtypeStruct((M, D), BF16),
        compiler_params=_params(1),
        name="norm_mod",
    )(*srcs, gain.reshape(1, D), mods.table, mods.table)


def _rms_body(x_ref, g_ref, o_ref):
    o_ref[...] = _rms(x_ref[...], g_ref[...]).astype(o_ref.dtype)


def final_norm(t, gain, m_rows, tm=512):
    D = t.shape[1]
    return pl.pallas_call(
        _rms_body,
        grid=(m_rows // tm,),
        in_specs=[pl.BlockSpec((tm, D), lambda i: (i, 0)), pl.BlockSpec((1, D), lambda i: (0, 0))],
        out_specs=pl.BlockSpec((tm, D), lambda i: (i, 0)),
        out_shape=jax.ShapeDtypeStruct((m_rows, D), F32),
        compiler_params=_params(1),
        name="final_norm",
    )(t, gain.reshape(1, D))


def _mm_body(a_ref, w_ref, o_ref):
    o_ref[...] = jnp.dot(a_ref[...], w_ref[...].astype(BF16), preferred_element_type=F32).astype(o_ref.dtype)


def matmul(a, w, wl, m_rows, n_cols, out_dtype, tm=1024, tn=512, name="mm"):
    K = w.shape[1]
    return pl.pallas_call(
        _mm_body,
        grid=(m_rows // tm, n_cols // tn),
        in_specs=[
            pl.BlockSpec((tm, K), lambda i, j: (i, 0)),
            pl.BlockSpec((None, K, tn), lambda i, j: (wl, 0, j)),
        ],
        out_specs=pl.BlockSpec((tm, tn), lambda i, j: (i, j)),
        out_shape=jax.ShapeDtypeStruct((m_rows, n_cols), out_dtype),
        compiler_params=_params(2),
        name=name,
    )(a, w)


def _mm_res_body(*refs, n_a, n_res, n0):
    a_refs, w_refs = refs[:n_a], refs[n_a : 2 * n_a]
    r_refs = refs[2 * n_a : 2 * n_a + n_res]
    g_ref, o_ref = refs[2 * n_a + n_res :]
    acc = jnp.dot(a_refs[0][...], w_refs[0][...].astype(BF16), preferred_element_type=F32)
    for a_ref, w_ref in zip(a_refs[1:], w_refs[1:]):
        acc = acc + jnp.dot(a_ref[...], w_ref[...].astype(BF16), preferred_element_type=F32)
    o_ref[...] = _from_source(r_refs, n0, pl.program_id(0)) + g_ref[0] * acc


def matmul_residual(a_parts, w, wl, res_srcs, mods, layer, which_gate, m_rows, tm=1024, tn=512, name="mm_res"):
    K, N = w.shape[1:]
    kp = K // len(a_parts)
    g_idx = mods.index(layer, which_gate, tm)
    n0, res_specs = _row_sources(res_srcs, tm, tn, lambda j: j)
    a_specs = [pl.BlockSpec((tm, kp), lambda i, j: (i, 0)) for _ in a_parts]
    w_specs = [pl.BlockSpec((None, kp, tn), lambda i, j, p=p: (wl, p, j)) for p in range(len(a_parts))]
    return pl.pallas_call(
        functools.partial(_mm_res_body, n_a=len(a_parts), n_res=len(res_srcs), n0=n0),
        grid=(m_rows // tm, N // tn),
        in_specs=a_specs + w_specs + res_specs + [pl.BlockSpec((1, 1, tn), lambda i, j: (g_idx(i), 0, j))],
        out_specs=pl.BlockSpec((tm, tn), lambda i, j: (i, j)),
        out_shape=jax.ShapeDtypeStruct((m_rows, N), F32),
        compiler_params=_params(2),
        name=name,
    )(*a_parts, *([w] * len(a_parts)), *res_srcs, mods.table)


def _swiglu_body(a_ref, wg_ref, wu_ref, wd_ref, o_ref, wd_bf_ref):
    a = a_ref[...]
    g = jnp.dot(a, wg_ref[...].astype(BF16), preferred_element_type=F32)
    u = jnp.dot(a, wu_ref[...].astype(BF16), preferred_element_type=F32)
    o_ref[...] = (_silu(g) * u).astype(o_ref.dtype)
    @pl.when(pl.program_id(0) == 0)
    def _():
        wd_bf_ref[...] = wd_ref[...].astype(BF16)


def matmul_swiglu(a, w_gu, w_down, wl, m_rows, tm=1024, tn=256):
    K, N2 = w_gu.shape[1:]
    H, Nd = w_down.shape[1:]
    assert N2 == 2 * H
    nj = H // tn
    wd_block = lambda i, j: jnp.where(i == 0, j, nj - 1)
    return pl.pallas_call(
        _swiglu_body,
        grid=(m_rows // tm, nj),
        in_specs=[
            pl.BlockSpec((tm, K), lambda i, j: (i, 0)),
            pl.BlockSpec((None, K, tn), lambda i, j: (wl, 0, j)),
            pl.BlockSpec((None, K, tn), lambda i, j: (wl, 0, j + nj)),
            pl.BlockSpec((None, tn, Nd), lambda i, j: (wl, wd_block(i, j), 0)),
        ],
        out_specs=[
            pl.BlockSpec((tm, tn), lambda i, j: (i, j)),
            pl.BlockSpec((None, tn, Nd), lambda i, j: (0, wd_block(i, j), 0)),
        ],
        out_shape=[jax.ShapeDtypeStruct((m_rows, H), BF16), jax.ShapeDtypeStruct((1, H, Nd), BF16)],
        compiler_params=_params(2),
        name="ffn_up",
    )(a, w_gu, w_gu, w_down)


def _pool_body(z_ref, w_ref, s_ref, o_ref, *, n_batch, ctx_len):
    b = pl.program_id(0)
    g = pl.program_id(1)
    rows, C = z_ref.shape

    def run(sub_len):
        n_sub = rows // sub_len
        stride = sub_len + POOL_HALO
        z = z_ref[...]
        gap = jnp.zeros((POOL_HALO, C), F32)
        zp = jnp.concatenate([p for k in range(n_sub) for p in (z[k * sub_len : (k + 1) * sub_len], gap)], axis=0)
        total = n_sub * stride
        t1 = lax.broadcasted_iota(jnp.int32, (sub_len, C), 0)
        t = t1 if n_sub == 1 else jnp.concatenate([t1] * n_sub, axis=0)

        def prev(a, k):
            return pltpu.roll(a, k, 0)

        def nxt(a, k):
            return pltpu.roll(a, total - k, 0)

        def window_sum(level):
            s = zp + prev(zp, 1)
            for lv in range(level):
                s = prev(s, 2**lv) + nxt(s, 2**lv)
            if n_sub == 1:
                return s[:sub_len]
            return jnp.concatenate([s[k * stride : k * stride + sub_len] for k in range(n_sub)], axis=0)

        for gi, window in enumerate(POOL_WINDOWS):

            @pl.when(g == gi)
            def _(gi=gi, window=window):
                half = window // 2
                cnt = (jnp.minimum(t - half + window, sub_len) - jnp.maximum(t - half, 0)).astype(F32)
                pooled = window_sum(gi) / cnt - z
                mixed = jnp.dot(pooled.astype(BF16), w_ref[0].astype(BF16), preferred_element_type=F32)
                o_ref[...] = (mixed * s_ref[...]).astype(o_ref.dtype)

    pl.when(b < n_batch)(lambda: run(rows))
    pl.when(b >= n_batch)(lambda: run(ctx_len))


def pool_mixer(z, pool_w, pool_scale, n_batch, seq, ctx_len):
    M = z.shape[0]
    n_groups, G, _ = pool_w.shape
    assert (M - n_batch * seq) % seq == 0 and seq % ctx_len == 0
    return pl.pallas_call(
        functools.partial(_pool_body, n_batch=n_batch, ctx_len=ctx_len),
        grid=(M // seq, n_groups),
        in_specs=[
            pl.BlockSpec((seq, G), lambda b, g: (b, g)),
            pl.BlockSpec((1, G, G), lambda b, g: (g, 0, 0)),
            pl.BlockSpec((1, G), lambda b, g: (0, g)),
        ],
        out_specs=pl.BlockSpec((seq, G), lambda b, g: (b, g)),
        out_shape=jax.ShapeDtypeStruct((M, n_groups * G), BF16),
        compiler_params=_params(2),
        name="pool_mixer",
    )(z, pool_w, pool_scale.reshape(1, n_groups * G))


def _sgu_body(u_ref, v_ref, ng_ref, ws_ref, bs_ref, o_ref):
    ws = ws_ref[0].astype(BF16)
    bs = bs_ref[0]
    ng = ng_ref[0]
    for c in range(u_ref.shape[0] // SGU_CHUNK):
        rows = pl.ds(c * SGU_CHUNK, SGU_CHUNK)
        vn = _rms(_gelu_tanh(v_ref[rows, :]), ng)
        gate = jnp.dot(ws, vn.astype(BF16), preferred_element_type=F32) + bs
        o_ref[rows, :] = (_gelu_tanh(u_ref[rows, :]) * gate).astype(o_ref.dtype)


def sgu_mixer(z, norm_g, w_s, b_s, col0, ts=1024):
    M = z.shape[0]
    H, P, _ = w_s.shape
    hd = norm_g.shape[1]
    cb0 = col0 // hd
    return pl.pallas_call(
        _sgu_body,
        grid=(M // ts, H),
        in_specs=[
            pl.BlockSpec((ts, hd), lambda i, h: (i, cb0 + h)),
            pl.BlockSpec((ts, hd), lambda i, h: (i, cb0 + H + h)),
            pl.BlockSpec((1, 1, hd), lambda i, h: (h, 0, 0)),
            pl.BlockSpec((1, P, P), lambda i, h: (h, 0, 0)),
            pl.BlockSpec((1, P, 1), lambda i, h: (h, 0, 0)),
        ],
        out_specs=pl.BlockSpec((ts, hd), lambda i, h: (i, h)),
        out_shape=jax.ShapeDtypeStruct((M, H * hd), BF16),
        compiler_params=_params(2),
        name="sgu_mixer",
    )(z, z, norm_g.reshape(H, 1, hd), w_s, b_s.reshape(H, P, 1))


def _rope_half(x, cos, sin_signed):
    return x * cos + pltpu.roll(x, LANE // 2, 1) * sin_signed


def _prep_q_body(zq_ref, qag_ref, qng_ref, cos_ref, sin_ref, cq_ref, gq_ref):
    cq_ref[...] = _rms(zq_ref[:, :MLA_Q_RANK], qag_ref[...]).astype(cq_ref.dtype)
    qs = _exp2_scale(1.0 / math.sqrt(GQA_HEAD_DIM))
    cos = cos_ref[...] * qs
    sin = sin_ref[...] * qs
    g = qng_ref[...]
    for h in range(GQA_HEADS):
        x = _rms(zq_ref[:, MLA_Q_RANK + h * LANE : MLA_Q_RANK + (h + 1) * LANE], g)
        gq_ref[:, h * LANE : (h + 1) * LANE] = _rope_half(x, cos, sin).astype(gq_ref.dtype)


def prep_q(zq, q_a_g, q_norm_g, cos_g, sin_g, tm=512):
    M, N = zq.shape
    nq = GQA_HEADS * GQA_HEAD_DIM
    return pl.pallas_call(
        _prep_q_body,
        grid=(M // tm,),
        in_specs=[
            pl.BlockSpec((tm, N), lambda i: (i, 0)),
            pl.BlockSpec((1, MLA_Q_RANK), lambda i: (0, 0)),
            pl.BlockSpec((1, LANE), lambda i: (0, 0)),
            pl.BlockSpec((tm, LANE), lambda i: (i, 0)),
            pl.BlockSpec((tm, LANE), lambda i: (i, 0)),
        ],
        out_specs=[pl.BlockSpec((tm, MLA_Q_RANK), lambda i: (i, 0)), pl.BlockSpec((tm, nq), lambda i: (i, 0))],
        out_shape=[jax.ShapeDtypeStruct((M, MLA_Q_RANK), BF16), jax.ShapeDtypeStruct((M, nq), BF16)],
        compiler_params=_params(1),
        name="prep_q",
    )(zq, q_a_g.reshape(1, -1), q_norm_g.reshape(1, -1), cos_g, sin_g)


def _prep_kv_body(z_ref, kag_ref, kng_ref, cg_ref, sg_ref, cm_ref, sm_ref, ckv_ref, kr_ref, gk_ref, gv_ref):
    ckv_ref[...] = _rms(z_ref[:, :MLA_KV_RANK], kag_ref[...]).astype(ckv_ref.dtype)
    kr = z_ref[:, MLA_KV_RANK : MLA_KV_RANK + LANE]
    lane = lax.broadcasted_iota(jnp.int32, kr.shape, 1)
    half = MLA_ROPE // 2
    swapped = jnp.where(lane < half, pltpu.roll(kr, LANE - half, 1), pltpu.roll(kr, half, 1))
    kr_ref[...] = (kr * cm_ref[...] + swapped * sm_ref[...]).astype(kr_ref.dtype)
    off_k = MLA_KV_RANK + LANE
    off_v = off_k + GQA_KV_HEADS * LANE
    cos = cg_ref[...]
    sin = sg_ref[...]
    g = kng_ref[...]
    for h in range(GQA_KV_HEADS):
        x = _rms(z_ref[:, off_k + h * LANE : off_k + (h + 1) * LANE], g)
        gk_ref[:, h * LANE : (h + 1) * LANE] = _rope_half(x, cos, sin).astype(gk_ref.dtype)
    gv_ref[...] = z_ref[:, off_v : off_v + GQA_KV_HEADS * LANE].astype(gv_ref.dtype)


def prep_kv(zkv, kv_a_g, k_norm_g, cos_g, sin_g, cos_m, sin_m, tm=512):
    M, N = zkv.shape
    nk = GQA_KV_HEADS * GQA_HEAD_DIM
    tok = lambda i: (i, 0)
    fixed = lambda i: (0, 0)
    return pl.pallas_call(
        _prep_kv_body,
        grid=(M // tm,),
        in_specs=[
            pl.BlockSpec((tm, N), tok),
            pl.BlockSpec((1, MLA_KV_RANK), fixed),
            pl.BlockSpec((1, LANE), fixed),
            pl.BlockSpec((tm, LANE), tok),
            pl.BlockSpec((tm, LANE), tok),
            pl.BlockSpec((tm, LANE), tok),
            pl.BlockSpec((tm, LANE), tok),
        ],
        out_specs=[
            pl.BlockSpec((tm, MLA_KV_RANK), tok),
            pl.BlockSpec((tm, LANE), tok),
            pl.BlockSpec((tm, nk), tok),
            pl.BlockSpec((tm, nk), tok),
        ],
        out_shape=[
            jax.ShapeDtypeStruct((M, MLA_KV_RANK), BF16),
            jax.ShapeDtypeStruct((M, LANE), BF16),
            jax.ShapeDtypeStruct((M, nk), BF16),
            jax.ShapeDtypeStruct((M, nk), BF16),
        ],
        compiler_params=_params(1),
        name="prep_kv",
    )(zkv, kv_a_g.reshape(1, -1), k_norm_g.reshape(1, -1), cos_g, sin_g, cos_m, sin_m)


def _mm_qrope_body(a_ref, w_ref, c_ref, s_ref, o_ref):
    acc = jnp.dot(a_ref[...], w_ref[...], preferred_element_type=F32) * _exp2_scale(
        1.0 / math.sqrt(MLA_NOPE + MLA_ROPE)
    )
    cos = c_ref[...]
    sin = s_ref[...]
    for h in range(acc.shape[1] // (2 * LANE)):
        lo = h * 2 * LANE
        o_ref[:, lo : lo + LANE] = acc[:, lo : lo + LANE].astype(o_ref.dtype)
        t = acc[:, lo + LANE : lo + 2 * LANE]
        o_ref[:, lo + LANE : lo + 2 * LANE] = (t * cos + pltpu.roll(t, LANE // 2, 1) * sin).astype(o_ref.dtype)


def matmul_qrope(a, w, cos_m, sin_m_abs, tm=1024, tn=1024):
    M, K = a.shape
    N = w.shape[1]
    return pl.pallas_call(
        _mm_qrope_body,
        grid=(M // tm, N // tn),
        in_specs=[
            pl.BlockSpec((tm, K), lambda i, j: (i, 0)),
            pl.BlockSpec((K, tn), lambda i, j: (0, j)),
            pl.BlockSpec((tm, LANE), lambda i, j: (i, 0)),
            pl.BlockSpec((tm, LANE), lambda i, j: (i, 0)),
        ],
        out_specs=pl.BlockSpec((tm, tn), lambda i, j: (i, j)),
        out_shape=jax.ShapeDtypeStruct((M, N), BF16),
        compiler_params=_params(2),
        name="mla_q_up",
    )(a, w, cos_m, sin_m_abs)


_NT = (((1,), (1,)), ((), ()))
ATTN_SUB = 256


def _exp2_scale(scale):
    return scale * math.log2(math.e)


def _attend(q_ref, kc, kx, vc, vx, o_ref, s_ref):
    n_c = kc.shape[0]
    n_sub = q_ref.shape[0] // ATTN_SUB

    def scores(r):
        slot = r % s_ref.shape[0]
        q = q_ref[pl.ds(r * ATTN_SUB, ATTN_SUB), :]
        s_ref[slot, :, :n_c] = lax.dot_general(q, kc, _NT, preferred_element_type=F32)
        s_ref[slot, :, n_c:] = lax.dot_general(q, kx, _NT, preferred_element_type=F32)
        s = s_ref[slot]
        return s, jnp.max(s, axis=-1, keepdims=True)

    nxt = scores(0)
    for r in range(n_sub):
        s, m = nxt
        p = jnp.exp2(s - m)
        denom = jnp.sum(p, axis=-1, keepdims=True)
        p_c = p[:, :n_c]
        if r + 1 < n_sub:
            nxt = scores(r + 1)
            p_c = p_c + _zero_after(nxt[1])
        o = jnp.dot(p_c.astype(BF16), vc, preferred_element_type=F32)
        o = o + jnp.dot(p[:, n_c:].astype(BF16), vx, preferred_element_type=F32)
        o_ref[pl.ds(r * ATTN_SUB, ATTN_SUB), :] = (o * (1.0 / denom)).astype(o_ref.dtype)


def _zero_after(x):
    bits = lax.bitcast_convert_type(x, jnp.uint32)
    return lax.bitcast_convert_type((bits >> 16) >> 16, F32)


def _score_scratch(n_keys):
    return [pltpu.VMEM((2, ATTN_SUB, n_keys), F32)]


def _mla_body(q_ref, knc_ref, knx_ref, krc_ref, krx_ref, vc_ref, vx_ref, o_ref, s_ref):
    kc = jnp.concatenate([knc_ref[...], krc_ref[...]], axis=1)
    kx = jnp.concatenate([knx_ref[...], krx_ref[...]], axis=1)
    _attend(q_ref, kc, kx, vc_ref[...], vx_ref[...], o_ref, s_ref)


def mla_attention(q, kv, kr, n_batch, seq, ctx_len, tq=2048):
    nq = seq // tq
    cb0 = n_batch * seq // ctx_len
    return pl.pallas_call(
        _mla_body,
        grid=(n_batch, MLA_HEADS, nq),
        in_specs=[
            pl.BlockSpec((tq, 2 * LANE), lambda b, h, i: (b * nq + i, h)),
            pl.BlockSpec((ctx_len, LANE), lambda b, h, i: (cb0 + b, 2 * h)),
            pl.BlockSpec((seq, LANE), lambda b, h, i: (b, 2 * h)),
            pl.BlockSpec((ctx_len, LANE), lambda b, h, i: (cb0 + b, 0)),
            pl.BlockSpec((seq, LANE), lambda b, h, i: (b, 0)),
            pl.BlockSpec((ctx_len, LANE), lambda b, h, i: (cb0 + b, 2 * h + 1)),
            pl.BlockSpec((seq, LANE), lambda b, h, i: (b, 2 * h + 1)),
        ],
        out_specs=pl.BlockSpec((tq, LANE), lambda b, h, i: (b * nq + i, h)),
        out_shape=jax.ShapeDtypeStruct((n_batch * seq, MLA_HEADS * MLA_V), BF16),
        scratch_shapes=_score_scratch(ctx_len + seq),
        compiler_params=_params(3),
        name="mla_attention",
    )(q, kv, kv, kr, kr, kv, kv)


def _gqa_body(q_ref, kc_ref, kx_ref, vc_ref, vx_ref, o_ref, s_ref):
    _attend(q_ref, kc_ref[...], kx_ref[...], vc_ref[...], vx_ref[...], o_ref, s_ref)


def gqa_attention(q, k, v, n_batch, seq, ctx_len, tq=2048):
    nq = seq // tq
    cb0 = n_batch * seq // ctx_len
    rep = GQA_HEADS // GQA_KV_HEADS
    return pl.pallas_call(
        _gqa_body,
        grid=(n_batch, GQA_HEADS, nq),
        in_specs=[
            pl.BlockSpec((tq, LANE), lambda b, h, i: (b * nq + i, h)),
            pl.BlockSpec((ctx_len, LANE), lambda b, h, i: (cb0 + b, h // rep)),
            pl.BlockSpec((seq, LANE), lambda b, h, i: (b, h // rep)),
            pl.BlockSpec((ctx_len, LANE), lambda b, h, i: (cb0 + b, h // rep)),
            pl.BlockSpec((seq, LANE), lambda b, h, i: (b, h // rep)),
        ],
        out_specs=pl.BlockSpec((tq, LANE), lambda b, h, i: (b * nq + i, h)),
        out_shape=jax.ShapeDtypeStruct((n_batch * seq, GQA_HEADS * GQA_HEAD_DIM), BF16),
        scratch_shapes=_score_scratch(ctx_len + seq),
        compiler_params=_params(3),
        name="gqa_attention",
    )(q, k, k, v, v)


def _rope_tables(n_batch, seq, ctx_len):
    rows = seq // GRID_W
    row = jnp.repeat(jnp.arange(rows, dtype=F32), GRID_W)
    col = jnp.tile(jnp.arange(GRID_W, dtype=F32), rows)

    def cos_sin(rot_dim):
        n_freq = rot_dim // 4
        inv = ROPE_THETA ** (-jnp.arange(n_freq, dtype=F32) / n_freq)
        ang = jnp.concatenate([row[:, None] * inv, col[:, None] * inv], axis=-1)
        return jnp.cos(ang), jnp.sin(ang)

    def over_tokens(x_part, ctx_value):
        x_rows = jnp.tile(x_part, (n_batch, 1))
        c_rows = jnp.broadcast_to(ctx_value, (n_batch * ctx_len, x_part.shape[1]))
        return jnp.concatenate([x_rows, c_rows], axis=0)

    cg, sg = cos_sin(GQA_HEAD_DIM)
    cos_g = over_tokens(jnp.concatenate([cg, cg], axis=1), jnp.ones((1, LANE), F32))
    sin_g = over_tokens(jnp.concatenate([-sg, sg], axis=1), jnp.zeros((1, LANE), F32))
    cm, sm = cos_sin(MLA_ROPE)
    zpad = jnp.zeros((seq, LANE - MLA_ROPE), F32)
    ident = jnp.concatenate([jnp.ones((1, MLA_ROPE), F32), jnp.zeros((1, LANE - MLA_ROPE), F32)], axis=1)
    cos_m = over_tokens(jnp.concatenate([cm, cm, zpad], axis=1), ident)
    sin_m = over_tokens(jnp.concatenate([-sm, sm, zpad], axis=1), jnp.zeros((1, LANE), F32))
    sin_m_abs = over_tokens(jnp.concatenate([sm, sm, zpad], axis=1), jnp.zeros((1, LANE), F32))
    return cos_g, sin_g, cos_m, sin_m, sin_m_abs


def _mla_q_weight(w_q_b):
    r = w_q_b.shape[0]
    w = w_q_b.reshape(r, MLA_HEADS, MLA_NOPE + MLA_ROPE)
    nope, rope = w[..., :MLA_NOPE], w[..., MLA_NOPE:]
    x1, x2 = rope[..., : MLA_ROPE // 2], rope[..., MLA_ROPE // 2 :]
    return jnp.concatenate([nope, rope, -x2, x1], axis=-1).reshape(r, MLA_HEADS * 2 * LANE).astype(BF16)


def _odd_kv_weight(w_in, q_cols):
    w = w_in[:, q_cols:]
    pad = jnp.zeros((w.shape[0], LANE - MLA_ROPE), w.dtype)
    split = MLA_KV_RANK + MLA_ROPE
    return jnp.concatenate([w[:, :split], pad, w[:, split:]], axis=1).astype(BF16)[None]


def kernel(x, c, ctx, c_ctx, mod_w, mod_b, norm1_g, norm2_g, final_g, ffn_w_gu, ffn_w_down, ev_w_in, ev_pool_w,
           ev_pool_scale, ev_sgu_norm_g, ev_sgu_w_s, ev_sgu_b, ev_w_out, od_w_in, od_q_a_g, od_w_q_b, od_kv_a_g,
           od_w_kv_b, od_q_norm_g, od_k_norm_g, od_w_out):
    B, S, D = x.shape
    CL = ctx.shape[1]
    depth = mod_w.shape[0]
    assert depth == 2, "layer schedule below is written for one even and one odd layer"
    n_x = B * S
    n_all = n_x + B * CL
    x_rows = x.reshape(n_x, D)
    c_rows = ctx.reshape(B * CL, D)

    cc = jnp.concatenate([c, c_ctx[None, :], jnp.zeros((MOD_ROWS - B - 1, D), F32)], axis=0)
    m = adaln_all(cc, mod_w, mod_b)
    table = m.reshape(depth, MOD_ROWS, 6, D).transpose(0, 2, 1, 3).reshape(depth * 6 * MOD_ROWS, 1, D)
    mods = Mods(table, B, S)
    def ffn(t_in, layer, m_rows):
        h = norm_mod([t_in], norm2_g[layer], mods, layer, SH2, SC2)
        act, w_down = matmul_swiglu(h, ffn_w_gu, ffn_w_down, layer, m_rows)
        return matmul_residual([act], w_down, 0, [t_in], mods, layer, G2, m_rows, tm=512, tn=512,
                               name="ffn_down")

    pool_width = ev_pool_scale.shape[1]
    h = norm_mod([x_rows, c_rows], norm1_g[0], mods, 0, SH1, SC1)
    z = matmul(h, ev_w_in, 0, n_all, ev_w_in.shape[2], F32, name="even_in")
    ya = pool_mixer(z, ev_pool_w[0], ev_pool_scale[0], B, S, CL)
    yb = sgu_mixer(z, ev_sgu_norm_g[0], ev_sgu_w_s[0], ev_sgu_b[0], pool_width)
    t = matmul_residual([ya, yb], ev_w_out, 0, [x_rows, c_rows], mods, 0, G1, n_all,
                        name="even_out")
    t = ffn(t, 0, n_all)

    q_cols = MLA_Q_RANK + GQA_HEADS * GQA_HEAD_DIM
    cos_g, sin_g, cos_m, sin_m, sin_m_abs = _rope_tables(B, S, CL)
    h = norm_mod([t], norm1_g[1], mods, 1, SH1, SC1)
    w_in_bf = od_w_in[0].astype(BF16)
    zq = matmul(h, w_in_bf[None, :, :q_cols], 0, n_x, q_cols, F32, name="odd_in_q")
    w_kv = _odd_kv_weight(w_in_bf, q_cols)
    zkv = matmul(h, w_kv, 0, n_all, w_kv.shape[2], F32, tm=512, tn=w_kv.shape[2], name="odd_in_kv")
    cq, gq = prep_q(zq, od_q_a_g[0], od_q_norm_g[0], cos_g, sin_g)
    ckv, kr, gk, gv = prep_kv(zkv, od_kv_a_g[0], od_k_norm_g[0], cos_g, sin_g, cos_m, sin_m)
    q_m = matmul_qrope(cq, _mla_q_weight(od_w_q_b[0]), cos_m, sin_m_abs)
    kv_m = matmul(ckv, od_w_kv_b, 0, n_all, od_w_kv_b.shape[2], BF16, tn=1024, name="mla_kv_up")
    o_m = mla_attention(q_m, kv_m, kr, B, S, CL)
    o_g = gqa_attention(gq, gk, gv, B, S, CL)
    t = matmul_residual([o_m, o_g], od_w_out, 0, [t], mods, 1, G1, n_x, name="odd_out")
    t = ffn(t, 1, n_x)

    return final_norm(t, final_g, n_x).reshape(B, S, D)
```

```python
import functools
import math

import jax
import jax.numpy as jnp
from jax import lax
from jax.experimental import pallas as pl
from jax.experimental.pallas import tpu as pltpu

F32 = jnp.float32
BF16 = jnp.bfloat16

GRID_W = 64
EPS = 1e-6
ROPE_THETA = 10000.0
POOL_WINDOWS = (2, 4, 8, 16)
POOL_HALO = 16
SGU_CHUNK = 128
MLA_HEADS = 16
MLA_Q_RANK = 1024
MLA_KV_RANK = 512
MLA_NOPE = 128
MLA_ROPE = 64
MLA_V = 128
GQA_HEADS = 16
GQA_KV_HEADS = 4
GQA_HEAD_DIM = 128
LANE = 128
MOD_ROWS = 16
SH1, SC1, G1, SH2, SC2, G2 = range(6)

VMEM_LIMIT = 56 * 1024 * 1024


def _params(n_axes):
    return pltpu.CompilerParams(dimension_semantics=("arbitrary",) * n_axes, vmem_limit_bytes=VMEM_LIMIT)


def _silu(x):
    return x / (1.0 + jnp.exp(-x))


def _gelu_tanh(x):
    return 0.5 * x * (1.0 + jnp.tanh(math.sqrt(2.0 / math.pi) * (x + 0.044715 * (x * x * x))))


def _rms(x, g):
    return x * lax.rsqrt(jnp.mean(x * x, axis=-1, keepdims=True) + EPS) * g


def _row_sources(srcs, tm, width, col_of):
    n0 = srcs[0].shape[0] // tm
    if len(srcs) == 1:
        return n0, [pl.BlockSpec((tm, width), lambda i, *r: (i, col_of(*r)))]
    n1 = srcs[1].shape[0] // tm
    return n0, [
        pl.BlockSpec((tm, width), lambda i, *r: (jnp.minimum(i, n0 - 1), col_of(*r))),
        pl.BlockSpec((tm, width), lambda i, *r: (jnp.clip(i - n0, 0, n1 - 1), col_of(*r))),
    ]


def _from_source(refs, n0, i):
    if len(refs) == 1:
        return refs[0][...]
    return jnp.where(i < n0, refs[0][...], refs[1][...])


def _adaln_body(c_ref, w_ref, b_ref, o_ref):
    sc = _silu(c_ref[...]).astype(BF16)
    m = jnp.dot(sc, w_ref[0].astype(BF16), preferred_element_type=F32) + b_ref[0]
    for r in range(MOD_ROWS):
        o_ref[r] = m[r : r + 1]


def adaln_all(cc, mod_w, mod_b, tn=1024):
    L, D, N = mod_w.shape
    per_chunk = D // tn
    return pl.pallas_call(
        _adaln_body,
        grid=(L, N // tn),
        in_specs=[
            pl.BlockSpec((MOD_ROWS, D), lambda l, j: (0, 0)),
            pl.BlockSpec((1, D, tn), lambda l, j: (l, 0, j)),
            pl.BlockSpec((1, 1, tn), lambda l, j: (l, 0, j)),
        ],
        out_specs=pl.BlockSpec((MOD_ROWS, 1, tn), lambda l, j: (l * (N // D) + j // per_chunk, 0, j % per_chunk)),
        out_shape=jax.ShapeDtypeStruct((L * (N // D) * MOD_ROWS, 1, D), F32),
        compiler_params=_params(2),
        name="adaln",
    )(cc, mod_w, mod_b.reshape(L, 1, N))


class Mods:
    def __init__(self, table, n_batch, seq):
        self.table = table
        self.n_batch = n_batch
        self.seq = seq

    def index(self, layer, which, tm):
        n_x = self.n_batch * self.seq // tm
        base = (layer * 6 + which) * MOD_ROWS
        return lambda i: base + jnp.where(i < n_x, (i * tm) // self.seq, self.n_batch)


def _norm_mod_body(*refs, n_src, n0):
    g_ref, sh_ref, sc_ref, o_ref = refs[n_src:]

    def emit(x_ref):
        y = _rms(x_ref[...], g_ref[...])
        o_ref[...] = (y * (1.0 + sc_ref[0]) + sh_ref[0]).astype(o_ref.dtype)

    if n_src == 1:
        emit(refs[0])
    else:
        i = pl.program_id(0)
        pl.when(i < n0)(lambda: emit(refs[0]))
        pl.when(i >= n0)(lambda: emit(refs[1]))


def norm_mod(srcs, gain, mods, layer, which_shift, which_scale, tm=512):
    D = srcs[0].shape[1]
    M = sum(s.shape[0] for s in srcs)
    sh_idx = mods.index(layer, which_shift, tm)
    sc_idx = mods.index(layer, which_scale, tm)
    n0, src_specs = _row_sources(srcs, tm, D, lambda: 0)
    return pl.pallas_call(
        functools.partial(_norm_mod_body, n_src=len(srcs), n0=n0),
        grid=(M // tm,),
        in_specs=src_specs
        + [
            pl.BlockSpec((1, D), lambda i: (0, 0)),
            pl.BlockSpec((1, 1, D), lambda i: (sh_idx(i), 0, 0)),
            pl.BlockSpec((1, 1, D), lambda i: (sc_idx(i), 0, 0)),
        ],
        out_specs=pl.BlockSpec((tm, D), lambda i: (i, 0)),
        out_shape=jax.ShapeDtypeStruct((M, D), BF16),
        compiler_params=_params(1),
        name="norm_mod",
    )(*srcs, gain.reshape(1, D), mods.table, mods.table)


def _rms_body(x_ref, g_ref, o_ref):
    o_ref[...] = _rms(x_ref[...], g_ref[...]).astype(o_ref.dtype)


def final_norm(t, gain, m_rows, tm=512):
    D = t.shape[1]
    return pl.pallas_call(
        _rms_body,
        grid=(m_rows // tm,),
        in_specs=[pl.BlockSpec((tm, D), lambda i: (i, 0)), pl.BlockSpec((1, D), lambda i: (0, 0))],
        out_specs=pl.BlockSpec((tm, D), lambda i: (i, 0)),
        out_shape=jax.ShapeDtypeStruct((m_rows, D), F32),
        compiler_params=_params(1),
        name="final_norm",
    )(t, gain.reshape(1, D))


def _mm_body(a_ref, w_ref, o_ref):
    o_ref[...] = jnp.dot(a_ref[...], w_ref[...].astype(BF16), preferred_element_type=F32).astype(o_ref.dtype)


def matmul(a, w, wl, m_rows, n_cols, out_dtype, tm=1024, tn=512, name="mm"):
    K = w.shape[1]
    return pl.pallas_call(
        _mm_body,
        grid=(m_rows // tm, n_cols // tn),
        in_specs=[
            pl.BlockSpec((tm, K), lambda i, j: (i, 0)),
            pl.BlockSpec((None, K, tn), lambda i, j: (wl, 0, j)),
        ],
        out_specs=pl.BlockSpec((tm, tn), lambda i, j: (i, j)),
        out_shape=jax.ShapeDtypeStruct((m_rows, n_cols), out_dtype),
        compiler_params=_params(2),
        name=name,
    )(a, w)


def _mm_res_body(*refs, n_a, n_res, n0):
    a_refs, w_refs = refs[:n_a], refs[n_a : 2 * n_a]
    r_refs = refs[2 * n_a : 2 * n_a + n_res]
    g_ref, o_ref = refs[2 * n_a + n_res :]
    acc = jnp.dot(a_refs[0][...], w_refs[0][...].astype(BF16), preferred_element_type=F32)
    for a_ref, w_ref in zip(a_refs[1:], w_refs[1:]):
        acc = acc + jnp.dot(a_ref[...], w_ref[...].astype(BF16), preferred_element_type=F32)
    o_ref[...] = _from_source(r_refs, n0, pl.program_id(0)) + g_ref[0] * acc


def matmul_residual(a_parts, w, wl, res_srcs, mods, layer, which_gate, m_rows, tm=1024, tn=512, name="mm_res"):
    K, N = w.shape[1:]
    kp = K // len(a_parts)
    g_idx = mods.index(layer, which_gate, tm)
    n0, res_specs = _row_sources(res_srcs, tm, tn, lambda j: j)
    a_specs = [pl.BlockSpec((tm, kp), lambda i, j: (i, 0)) for _ in a_parts]
    w_specs = [pl.BlockSpec((None, kp, tn), lambda i, j, p=p: (wl, p, j)) for p in range(len(a_parts))]
    return pl.pallas_call(
        functools.partial(_mm_res_body, n_a=len(a_parts), n_res=len(res_srcs), n0=n0),
        grid=(m_rows // tm, N // tn),
        in_specs=a_specs + w_specs + res_specs + [pl.BlockSpec((1, 1, tn), lambda i, j: (g_idx(i), 0, j))],
        out_specs=pl.BlockSpec((tm, tn), lambda i, j: (i, j)),
        out_shape=jax.ShapeDtypeStruct((m_rows, N), F32),
        compiler_params=_params(2),
        name=name,
    )(*a_parts, *([w] * len(a_parts)), *res_srcs, mods.table)


def _swiglu_body(a_ref, wg_ref, wu_ref, wd_ref, o_ref, wd_bf_ref):
    a = a_ref[...]
    g = jnp.dot(a, wg_ref[...].astype(BF16), preferred_element_type=F32)
    u = jnp.dot(a, wu_ref[...].astype(BF16), preferred_element_type=F32)
    o_ref[...] = (_silu(g) * u).astype(o_ref.dtype)
    @pl.when(pl.program_id(0) == 0)
    def _():
        wd_bf_ref[...] = wd_ref[...].astype(BF16)


def matmul_swiglu(a, w_gu, w_down, wl, m_rows, tm=1024, tn=256):
    K, N2 = w_gu.shape[1:]
    H, Nd = w_down.shape[1:]
    assert N2 == 2 * H
    nj = H // tn
    wd_block = lambda i, j: jnp.where(i == 0, j, nj - 1)
    return pl.pallas_call(
        _swiglu_body,
        grid=(m_rows // tm, nj),
        in_specs=[
            pl.BlockSpec((tm, K), lambda i, j: (i, 0)),
            pl.BlockSpec((None, K, tn), lambda i, j: (wl, 0, j)),
            pl.BlockSpec((None, K, tn), lambda i, j: (wl, 0, j + nj)),
            pl.BlockSpec((None, tn, Nd), lambda i, j: (wl, wd_block(i, j), 0)),
        ],
        out_specs=[
            pl.BlockSpec((tm, tn), lambda i, j: (i, j)),
            pl.BlockSpec((None, tn, Nd), lambda i, j: (0, wd_block(i, j), 0)),
        ],
        out_shape=[jax.ShapeDtypeStruct((m_rows, H), BF16), jax.ShapeDtypeStruct((1, H, Nd), BF16)],
        compiler_params=_params(2),
        name="ffn_up",
    )(a, w_gu, w_gu, w_down)


def _pool_body(z_ref, w_ref, s_ref, o_ref, *, n_batch, ctx_len):
    b = pl.program_id(0)
    g = pl.program_id(1)
    rows, C = z_ref.shape

    def run(sub_len):
        n_sub = rows // sub_len
        stride = sub_len + POOL_HALO
        z = z_ref[...]
        gap = jnp.zeros((POOL_HALO, C), F32)
        zp = jnp.concatenate([p for k in range(n_sub) for p in (z[k * sub_len : (k + 1) * sub_len], gap)], axis=0)
        total = n_sub * stride
        t1 = lax.broadcasted_iota(jnp.int32, (sub_len, C), 0)
        t = t1 if n_sub == 1 else jnp.concatenate([t1] * n_sub, axis=0)

        def prev(a, k):
            return pltpu.roll(a, k, 0)

        def nxt(a, k):
            return pltpu.roll(a, total - k, 0)

        def window_sum(level):
            s = zp + prev(zp, 1)
            for lv in range(level):
                s = prev(s, 2**lv) + nxt(s, 2**lv)
            if n_sub == 1:
                return s[:sub_len]
            return jnp.concatenate([s[k * stride : k * stride + sub_len] for k in range(n_sub)], axis=0)

        for gi, window in enumerate(POOL_WINDOWS):

            @pl.when(g == gi)
            def _(gi=gi, window=window):
                half = window // 2
                cnt = (jnp.minimum(t - half + window, sub_len) - jnp.maximum(t - half, 0)).astype(F32)
                pooled = window_sum(gi) / cnt - z
                mixed = jnp.dot(pooled.astype(BF16), w_ref[0].astype(BF16), preferred_element_type=F32)
                o_ref[...] = (mixed * s_ref[...]).astype(o_ref.dtype)

    pl.when(b < n_batch)(lambda: run(rows))
    pl.when(b >= n_batch)(lambda: run(ctx_len))


def pool_mixer(z, pool_w, pool_scale, n_batch, seq, ctx_len):
    M = z.shape[0]
    n_groups, G, _ = pool_w.shape
    assert (M - n_batch * seq) % seq == 0 and seq % ctx_len == 0
    return pl.pallas_call(
        functools.partial(_pool_body, n_batch=n_batch, ctx_len=ctx_len),
        grid=(M // seq, n_groups),
        in_specs=[
            pl.BlockSpec((seq, G), lambda b, g: (b, g)),
            pl.BlockSpec((1, G, G), lambda b, g: (g, 0, 0)),
            pl.BlockSpec((1, G), lambda b, g: (0, g)),
        ],
        out_specs=pl.BlockSpec((seq, G), lambda b, g: (b, g)),
        out_shape=jax.ShapeDtypeStruct((M, n_groups * G), BF16),
        compiler_params=_params(2),
        name="pool_mixer",
    )(z, pool_w, pool_scale.reshape(1, n_groups * G))


def _sgu_body(u_ref, v_ref, ng_ref, ws_ref, bs_ref, o_ref):
    ws = ws_ref[0].astype(BF16)
    bs = bs_ref[0]
    ng = ng_ref[0]
    for c in range(u_ref.shape[0] // SGU_CHUNK):
        rows = pl.ds(c * SGU_CHUNK, SGU_CHUNK)
        vn = _rms(_gelu_tanh(v_ref[rows, :]), ng)
        gate = jnp.dot(ws, vn.astype(BF16), preferred_element_type=F32) + bs
        o_ref[rows, :] = (_gelu_tanh(u_ref[rows, :]) * gate).astype(o_ref.dtype)


def sgu_mixer(z, norm_g, w_s, b_s, col0, ts=2048):
    M = z.shape[0]
    H, P, _ = w_s.shape
    hd = norm_g.shape[1]
    cb0 = col0 // hd
    return pl.pallas_call(
        _sgu_body,
        grid=(M // ts, H),
        in_specs=[
            pl.BlockSpec((ts, hd), lambda i, h: (i, cb0 + h)),
            pl.BlockSpec((ts, hd), lambda i, h: (i, cb0 + H + h)),
            pl.BlockSpec((1, 1, hd), lambda i, h: (h, 0, 0)),
            pl.BlockSpec((1, P, P), lambda i, h: (h, 0, 0)),
            pl.BlockSpec((1, P, 1), lambda i, h: (h, 0, 0)),
        ],
        out_specs=pl.BlockSpec((ts, hd), lambda i, h: (i, h)),
        out_shape=jax.ShapeDtypeStruct((M, H * hd), BF16),
        compiler_params=_params(2),
        name="sgu_mixer",
    )(z, z, norm_g.reshape(H, 1, hd), w_s, b_s.reshape(H, P, 1))


def _rope_half(x, cos, sin_signed):
    return x * cos + pltpu.roll(x, LANE // 2, 1) * sin_signed


def _prep_q_body(zq_ref, qag_ref, qng_ref, cos_ref, sin_ref, cq_ref, gq_ref):
    cq_ref[...] = _rms(zq_ref[:, :MLA_Q_RANK], qag_ref[...]).astype(cq_ref.dtype)
    qs = _exp2_scale(1.0 / math.sqrt(GQA_HEAD_DIM))
    cos = cos_ref[...] * qs
    sin = sin_ref[...] * qs
    g = qng_ref[...]
    for h in range(GQA_HEADS):
        x = _rms(zq_ref[:, MLA_Q_RANK + h * LANE : MLA_Q_RANK + (h + 1) * LANE], g)
        gq_ref[:, h * LANE : (h + 1) * LANE] = _rope_half(x, cos, sin).astype(gq_ref.dtype)


def prep_q(zq, q_a_g, q_norm_g, cos_g, sin_g, tm=512):
    M, N = zq.shape
    nq = GQA_HEADS * GQA_HEAD_DIM
    return pl.pallas_call(
        _prep_q_body,
        grid=(M // tm,),
        in_specs=[
            pl.BlockSpec((tm, N), lambda i: (i, 0)),
            pl.BlockSpec((1, MLA_Q_RANK), lambda i: (0, 0)),
            pl.BlockSpec((1, LANE), lambda i: (0, 0)),
            pl.BlockSpec((tm, LANE), lambda i: (i, 0)),
            pl.BlockSpec((tm, LANE), lambda i: (i, 0)),
        ],
        out_specs=[pl.BlockSpec((tm, MLA_Q_RANK), lambda i: (i, 0)), pl.BlockSpec((tm, nq), lambda i: (i, 0))],
        out_shape=[jax.ShapeDtypeStruct((M, MLA_Q_RANK), BF16), jax.ShapeDtypeStruct((M, nq), BF16)],
        compiler_params=_params(1),
        name="prep_q",
    )(zq, q_a_g.reshape(1, -1), q_norm_g.reshape(1, -1), cos_g, sin_g)


def _prep_kv_body(z_ref, kag_ref, kng_ref, cg_ref, sg_ref, cm_ref, sm_ref, ckv_ref, kr_ref, gk_ref, gv_ref):
    ckv_ref[...] = _rms(z_ref[:, :MLA_KV_RANK], kag_ref[...]).astype(ckv_ref.dtype)
    kr = z_ref[:, MLA_KV_RANK : MLA_KV_RANK + LANE]
    lane = lax.broadcasted_iota(jnp.int32, kr.shape, 1)
    half = MLA_ROPE // 2
    swapped = jnp.where(lane < half, pltpu.roll(kr, LANE - half, 1), pltpu.roll(kr, half, 1))
    kr_ref[...] = (kr * cm_ref[...] + swapped * sm_ref[...]).astype(kr_ref.dtype)
    off_k = MLA_KV_RANK + LANE
    off_v = off_k + GQA_KV_HEADS * LANE
    cos = cg_ref[...]
    sin = sg_ref[...]
    g = kng_ref[...]
    for h in range(GQA_KV_HEADS):
        x = _rms(z_ref[:, off_k + h * LANE : off_k + (h + 1) * LANE], g)
        gk_ref[:, h * LANE : (h + 1) * LANE] = _rope_half(x, cos, sin).astype(gk_ref.dtype)
    gv_ref[...] = z_ref[:, off_v : off_v + GQA_KV_HEADS * LANE].astype(gv_ref.dtype)


def prep_kv(zkv, kv_a_g, k_norm_g, cos_g, sin_g, cos_m, sin_m, tm=512):
    M, N = zkv.shape
    nk = GQA_KV_HEADS * GQA_HEAD_DIM
    tok = lambda i: (i, 0)
    fixed = lambda i: (0, 0)
    return pl.pallas_call(
        _prep_kv_body,
        grid=(M // tm,),
        in_specs=[
            pl.BlockSpec((tm, N), tok),
            pl.BlockSpec((1, MLA_KV_RANK), fixed),
            pl.BlockSpec((1, LANE), fixed),
            pl.BlockSpec((tm, LANE), tok),
            pl.BlockSpec((tm, LANE), tok),
            pl.BlockSpec((tm, LANE), tok),
            pl.BlockSpec((tm, LANE), tok),
        ],
        out_specs=[
            pl.BlockSpec((tm, MLA_KV_RANK), tok),
            pl.BlockSpec((tm, LANE), tok),
            pl.BlockSpec((tm, nk), tok),
            pl.BlockSpec((tm, nk), tok),
        ],
        out_shape=[
            jax.ShapeDtypeStruct((M, MLA_KV_RANK), BF16),
            jax.ShapeDtypeStruct((M, LANE), BF16),
            jax.ShapeDtypeStruct((M, nk), BF16),
            jax.ShapeDtypeStruct((M, nk), BF16),
        ],
        compiler_params=_params(1),
        name="prep_kv",
    )(zkv, kv_a_g.reshape(1, -1), k_norm_g.reshape(1, -1), cos_g, sin_g, cos_m, sin_m)


def _mm_qrope_body(a_ref, w_ref, c_ref, s_ref, o_ref):
    acc = jnp.dot(a_ref[...], w_ref[...], preferred_element_type=F32) * _exp2_scale(
        1.0 / math.sqrt(MLA_NOPE + MLA_ROPE)
    )
    cos = c_ref[...]
    sin = s_ref[...]
    for h in range(acc.shape[1] // (2 * LANE)):
        lo = h * 2 * LANE
        o_ref[:, lo : lo + LANE] = acc[:, lo : lo + LANE].astype(o_ref.dtype)
        t = acc[:, lo + LANE : lo + 2 * LANE]
        o_ref[:, lo + LANE : lo + 2 * LANE] = (t * cos + pltpu.roll(t, LANE // 2, 1) * sin).astype(o_ref.dtype)


def matmul_qrope(a, w, cos_m, sin_m_abs, tm=1024, tn=1024):
    M, K = a.shape
    N = w.shape[1]
    return pl.pallas_call(
        _mm_qrope_body,
        grid=(M // tm, N // tn),
        in_specs=[
            pl.BlockSpec((tm, K), lambda i, j: (i, 0)),
            pl.BlockSpec((K, tn), lambda i, j: (0, j)),
            pl.BlockSpec((tm, LANE), lambda i, j: (i, 0)),
            pl.BlockSpec((tm, LANE), lambda i, j: (i, 0)),
        ],
        out_specs=pl.BlockSpec((tm, tn), lambda i, j: (i, j)),
        out_shape=jax.ShapeDtypeStruct((M, N), BF16),
        compiler_params=_params(2),
        name="mla_q_up",
    )(a, w, cos_m, sin_m_abs)


_NT = (((1,), (1,)), ((), ()))
ATTN_SUB = 256


def _exp2_scale(scale):
    return scale * math.log2(math.e)


def _attend(q_ref, o_ref, s_ref, heads):
    n_sub = q_ref.shape[0] // ATTN_SUB
    chains = [(head, r) for head in heads for r in range(n_sub)]

    def scores(idx):
        (q_cols, _, kc, kx, _, _), r = chains[idx]
        slot = idx % s_ref.shape[0]
        n_c = kc.shape[0]
        q = q_ref[pl.ds(r * ATTN_SUB, ATTN_SUB), q_cols]
        s_ref[slot, :, :n_c] = lax.dot_general(q, kc, _NT, preferred_element_type=F32)
        s_ref[slot, :, n_c:] = lax.dot_general(q, kx, _NT, preferred_element_type=F32)
        s = s_ref[slot]
        return s, jnp.max(s, axis=-1, keepdims=True)

    nxt = scores(0)
    for idx, ((_, o_cols, kc, _, vc, vx), r) in enumerate(chains):
        n_c = kc.shape[0]
        s, m = nxt
        p = jnp.exp2(s - m)
        denom = jnp.sum(p, axis=-1, keepdims=True)
        p_c = p[:, :n_c]
        if idx + 1 < len(chains):
            nxt = scores(idx + 1)
            p_c = p_c + _zero_after(nxt[1])
        o = jnp.dot(p_c.astype(BF16), vc, preferred_element_type=F32)
        o = o + jnp.dot(p[:, n_c:].astype(BF16), vx, preferred_element_type=F32)
        o_ref[pl.ds(r * ATTN_SUB, ATTN_SUB), o_cols] = (o * (1.0 / denom)).astype(o_ref.dtype)


def _zero_after(x):
    bits = lax.bitcast_convert_type(x, jnp.uint32)
    return lax.bitcast_convert_type((bits >> 16) >> 16, F32)


def _score_scratch(n_keys):
    return [pltpu.VMEM((2, ATTN_SUB, n_keys), F32)]


MLA_STEP_HEADS = 1


def _mla_body(q_ref, kvc_ref, kvx_ref, krc_ref, krx_ref, o_ref, s_ref):
    krc = krc_ref[...]
    krx = krx_ref[...]
    heads = []
    for h in range(MLA_STEP_HEADS):
        kn = slice(2 * h * LANE, (2 * h + 1) * LANE)
        v = slice((2 * h + 1) * LANE, (2 * h + 2) * LANE)
        heads.append((
            slice(2 * h * LANE, (2 * h + 2) * LANE),
            slice(h * LANE, (h + 1) * LANE),
            jnp.concatenate([kvc_ref[:, kn], krc], axis=1),
            jnp.concatenate([kvx_ref[:, kn], krx], axis=1),
            kvc_ref[:, v],
            kvx_ref[:, v],
        ))
    _attend(q_ref, o_ref, s_ref, heads)


def mla_attention(q, kv, kr, n_batch, seq, ctx_len):
    cb0 = n_batch * seq // ctx_len
    hw = MLA_STEP_HEADS * 2 * LANE
    return pl.pallas_call(
        _mla_body,
        grid=(n_batch, MLA_HEADS // MLA_STEP_HEADS),
        in_specs=[
            pl.BlockSpec((seq, hw), lambda b, g: (b, g)),
            pl.BlockSpec((ctx_len, hw), lambda b, g: (cb0 + b, g)),
            pl.BlockSpec((seq, hw), lambda b, g: (b, g)),
            pl.BlockSpec((ctx_len, LANE), lambda b, g: (cb0 + b, 0)),
            pl.BlockSpec((seq, LANE), lambda b, g: (b, 0)),
        ],
        out_specs=pl.BlockSpec((seq, MLA_STEP_HEADS * LANE), lambda b, g: (b, g)),
        out_shape=jax.ShapeDtypeStruct((n_batch * seq, MLA_HEADS * MLA_V), BF16),
        scratch_shapes=_score_scratch(ctx_len + seq),
        compiler_params=_params(2),
        name="mla_attention",
    )(q, kv, kv, kr, kr)


def _gqa_body(q_ref, kc_ref, kx_ref, vc_ref, vx_ref, o_ref, s_ref):
    kc, kx, vc, vx = kc_ref[...], kx_ref[...], vc_ref[...], vx_ref[...]
    cols = [slice(h * LANE, (h + 1) * LANE) for h in range(q_ref.shape[1] // LANE)]
    _attend(q_ref, o_ref, s_ref, [(c, c, kc, kx, vc, vx) for c in cols])


GQA_STEP_HEADS = 1


def gqa_attention(q, k, v, n_batch, seq, ctx_len):
    cb0 = n_batch * seq // ctx_len
    qw = GQA_STEP_HEADS * LANE
    per_kv = GQA_HEADS // GQA_KV_HEADS // GQA_STEP_HEADS
    return pl.pallas_call(
        _gqa_body,
        grid=(n_batch, GQA_HEADS // GQA_STEP_HEADS),
        in_specs=[
            pl.BlockSpec((seq, qw), lambda b, g: (b, g)),
            pl.BlockSpec((ctx_len, LANE), lambda b, g: (cb0 + b, g // per_kv)),
            pl.BlockSpec((seq, LANE), lambda b, g: (b, g // per_kv)),
            pl.BlockSpec((ctx_len, LANE), lambda b, g: (cb0 + b, g // per_kv)),
            pl.BlockSpec((seq, LANE), lambda b, g: (b, g // per_kv)),
        ],
        out_specs=pl.BlockSpec((seq, qw), lambda b, g: (b, g)),
        out_shape=jax.ShapeDtypeStruct((n_batch * seq, GQA_HEADS * GQA_HEAD_DIM), BF16),
        scratch_shapes=_score_scratch(ctx_len + seq),
        compiler_params=_params(2),
        name="gqa_attention",
    )(q, k, k, v, v)


def _rope_tables(n_batch, seq, ctx_len):
    rows = seq // GRID_W
    row = jnp.repeat(jnp.arange(rows, dtype=F32), GRID_W)
    col = jnp.tile(jnp.arange(GRID_W, dtype=F32), rows)

    def cos_sin(rot_dim):
        n_freq = rot_dim // 4
        inv = ROPE_THETA ** (-jnp.arange(n_freq, dtype=F32) / n_freq)
        ang = jnp.concatenate([row[:, None] * inv, col[:, None] * inv], axis=-1)
        return jnp.cos(ang), jnp.sin(ang)

    def over_tokens(x_part, ctx_value):
        x_rows = jnp.tile(x_part, (n_batch, 1))
        c_rows = jnp.broadcast_to(ctx_value, (n_batch * ctx_len, x_part.shape[1]))
        return jnp.concatenate([x_rows, c_rows], axis=0)

    cg, sg = cos_sin(GQA_HEAD_DIM)
    cos_g = over_tokens(jnp.concatenate([cg, cg], axis=1), jnp.ones((1, LANE), F32))
    sin_g = over_tokens(jnp.concatenate([-sg, sg], axis=1), jnp.zeros((1, LANE), F32))
    cm, sm = cos_sin(MLA_ROPE)
    zpad = jnp.zeros((seq, LANE - MLA_ROPE), F32)
    ident = jnp.concatenate([jnp.ones((1, MLA_ROPE), F32), jnp.zeros((1, LANE - MLA_ROPE), F32)], axis=1)
    cos_m = over_tokens(jnp.concatenate([cm, cm, zpad], axis=1), ident)
    sin_m = over_tokens(jnp.concatenate([-sm, sm, zpad], axis=1), jnp.zeros((1, LANE), F32))
    sin_m_abs = over_tokens(jnp.concatenate([sm, sm, zpad], axis=1), jnp.zeros((1, LANE), F32))
    return cos_g, sin_g, cos_m, sin_m, sin_m_abs


def _mla_q_weight(w_q_b):
    r = w_q_b.shape[0]
    w = w_q_b.reshape(r, MLA_HEADS, MLA_NOPE + MLA_ROPE)
    nope, rope = w[..., :MLA_NOPE], w[..., MLA_NOPE:]
    x1, x2 = rope[..., : MLA_ROPE // 2], rope[..., MLA_ROPE // 2 :]
    return jnp.concatenate([nope, rope, -x2, x1], axis=-1).reshape(r, MLA_HEADS * 2 * LANE).astype(BF16)


def _odd_kv_weight(w_in, q_cols):
    w = w_in[:, q_cols:]
    pad = jnp.zeros((w.shape[0], LANE - MLA_ROPE), w.dtype)
    split = MLA_KV_RANK + MLA_ROPE
    return jnp.concatenate([w[:, :split], pad, w[:, split:]], axis=1).astype(BF16)[None]


def kernel(x, c, ctx, c_ctx, mod_w, mod_b, norm1_g, norm2_g, final_g, ffn_w_gu, ffn_w_down, ev_w_in, ev_pool_w,
           ev_pool_scale, ev_sgu_norm_g, ev_sgu_w_s, ev_sgu_b, ev_w_out, od_w_in, od_q_a_g, od_w_q_b, od_kv_a_g,
           od_w_kv_b, od_q_norm_g, od_k_norm_g, od_w_out):
    B, S, D = x.shape
    CL = ctx.shape[1]
    depth = mod_w.shape[0]
    assert depth == 2, "layer schedule below is written for one even and one odd layer"
    n_x = B * S
    n_all = n_x + B * CL
    x_rows = x.reshape(n_x, D)
    c_rows = ctx.reshape(B * CL, D)

    cc = jnp.concatenate([c, c_ctx[None, :], jnp.zeros((MOD_ROWS - B - 1, D), F32)], axis=0)
    mods = Mods(adaln_all(cc, mod_w, mod_b), B, S)

    def ffn(t_in, layer, m_rows):
        h = norm_mod([t_in], norm2_g[layer], mods, layer, SH2, SC2)
        act, w_down = matmul_swiglu(h, ffn_w_gu, ffn_w_down, layer, m_rows)
        return matmul_residual([act], w_down, 0, [t_in], mods, layer, G2, m_rows, tm=512, tn=512,
                               name="ffn_down")

    pool_width = ev_pool_scale.shape[1]
    h = norm_mod([x_rows, c_rows], norm1_g[0], mods, 0, SH1, SC1)
    z = matmul(h, ev_w_in, 0, n_all, ev_w_in.shape[2], F32, name="even_in")
    ya = pool_mixer(z, ev_pool_w[0], ev_pool_scale[0], B, S, CL)
    yb = sgu_mixer(z, ev_sgu_norm_g[0], ev_sgu_w_s[0], ev_sgu_b[0], pool_width)
    t = matmul_residual([ya, yb], ev_w_out.astype(BF16), 0, [x_rows, c_rows], mods, 0, G1, n_all,
                        name="even_out")
    t = ffn(t, 0, n_all)

    q_cols = MLA_Q_RANK + GQA_HEADS * GQA_HEAD_DIM
    cos_g, sin_g, cos_m, sin_m, sin_m_abs = _rope_tables(B, S, CL)
    h = norm_mod([t], norm1_g[1], mods, 1, SH1, SC1)
    w_in_bf = od_w_in[0].astype(BF16)
    zq = matmul(h, w_in_bf[None, :, :q_cols], 0, n_x, q_cols, F32, name="odd_in_q")
    w_kv = _odd_kv_weight(w_in_bf, q_cols)
    zkv = matmul(h, w_kv, 0, n_all, w_kv.shape[2], F32, tm=512, tn=w_kv.shape[2], name="odd_in_kv")
    cq, gq = prep_q(zq, od_q_a_g[0], od_q_norm_g[0], cos_g, sin_g)
    ckv, kr, gk, gv = prep_kv(zkv, od_kv_a_g[0], od_k_norm_g[0], cos_g, sin_g, cos_m, sin_m)
    w_q_up = _mla_q_weight(od_w_q_b[0])
    q_m = matmul_qrope(cq, w_q_up, cos_m, sin_m_abs, tm=512, tn=w_q_up.shape[1])
    kv_m = matmul(ckv, od_w_kv_b, 0, n_all, od_w_kv_b.shape[2], BF16, tm=512, tn=od_w_kv_b.shape[2],
                  name="mla_kv_up")
    o_m = mla_attention(q_m, kv_m, kr, B, S, CL)
    o_g = gqa_attention(gq, gk, gv, B, S, CL)
    t = matmul_residual([o_m, o_g], od_w_out, 0, [t], mods, 1, G1, n_x, name="odd_out")
    t = ffn(t, 1, n_x)

    return final_norm(t, final_g, n_x).reshape(B, S, D)
```

```python
import functools
import math

import jax
import jax.numpy as jnp
import numpy as np
from jax import lax
from jax.experimental import pallas as pl
from jax.experimental.pallas import tpu as pltpu

F32 = jnp.float32
BF16 = jnp.bfloat16

GRID_W = 64
EPS = 1e-6
ROPE_THETA = 10000.0
POOL_WINDOWS = (2, 4, 8, 16)
POOL_HALO = 16
SGU_CHUNK = 128
MLA_HEADS = 16
MLA_Q_RANK = 1024
MLA_KV_RANK = 512
MLA_NOPE = 128
MLA_ROPE = 64
MLA_V = 128
GQA_HEADS = 16
GQA_KV_HEADS = 4
GQA_HEAD_DIM = 128
LANE = 128
MOD_ROWS = 16
SH1, SC1, G1, SH2, SC2, G2 = range(6)

VMEM_LIMIT = 56 * 1024 * 1024


def _params(n_axes):
    return pltpu.CompilerParams(dimension_semantics=("arbitrary",) * n_axes, vmem_limit_bytes=VMEM_LIMIT)


def _silu(x):
    return x / (1.0 + jnp.exp(-x))


def _gelu_tanh(x):
    return 0.5 * x * (1.0 + jnp.tanh(math.sqrt(2.0 / math.pi) * (x + 0.044715 * (x * x * x))))


def _rms(x, g):
    return x * lax.rsqrt(jnp.mean(x * x, axis=-1, keepdims=True) + EPS) * g


def _row_sources(srcs, tm, width, col_of):
    n0 = srcs[0].shape[0] // tm
    if len(srcs) == 1:
        return n0, [pl.BlockSpec((tm, width), lambda i, *r: (i, col_of(*r)))]
    n1 = srcs[1].shape[0] // tm
    return n0, [
        pl.BlockSpec((tm, width), lambda i, *r: (jnp.minimum(i, n0 - 1), col_of(*r))),
        pl.BlockSpec((tm, width), lambda i, *r: (jnp.clip(i - n0, 0, n1 - 1), col_of(*r))),
    ]


def _from_source(refs, n0, i):
    if len(refs) == 1:
        return refs[0][...]
    return jnp.where(i < n0, refs[0][...], refs[1][...])


def _adaln_body(c_ref, w_ref, b_ref, o_ref):
    sc = _silu(c_ref[...]).astype(BF16)
    m = jnp.dot(sc, w_ref[0].astype(BF16), preferred_element_type=F32) + b_ref[0]
    for r in range(MOD_ROWS):
        o_ref[r] = m[r : r + 1]


def adaln_all(cc, mod_w, mod_b, tn=1024):
    L, D, N = mod_w.shape
    per_chunk = D // tn
    return pl.pallas_call(
        _adaln_body,
        grid=(L, N // tn),
        in_specs=[
            pl.BlockSpec((MOD_ROWS, D), lambda l, j: (0, 0)),
            pl.BlockSpec((1, D, tn), lambda l, j: (l, 0, j)),
            pl.BlockSpec((1, 1, tn), lambda l, j: (l, 0, j)),
        ],
        out_specs=pl.BlockSpec((MOD_ROWS, 1, tn), lambda l, j: (l * (N // D) + j // per_chunk, 0, j % per_chunk)),
        out_shape=jax.ShapeDtypeStruct((L * (N // D) * MOD_ROWS, 1, D), F32),
        compiler_params=_params(2),
        name="adaln",
    )(cc, mod_w, mod_b.reshape(L, 1, N))


class Mods:
    def __init__(self, table, n_batch, seq):
        self.table = table
        self.n_batch = n_batch
        self.seq = seq

    def index(self, layer, which, tm):
        n_x = self.n_batch * self.seq // tm
        base = (layer * 6 + which) * MOD_ROWS
        return lambda i: base + jnp.where(i < n_x, (i * tm) // self.seq, self.n_batch)


def _norm_mod_body(*refs, n_src, n0):
    g_ref, sh_ref, sc_ref, o_ref = refs[n_src:]

    def emit(x_ref):
        o_ref[...] = (_rms(x_ref[...], g_ref[...] * (1.0 + sc_ref[0])) + sh_ref[0]).astype(o_ref.dtype)

    if n_src == 1:
        emit(refs[0])
    else:
        i = pl.program_id(0)
        pl.when(i < n0)(lambda: emit(refs[0]))
        pl.when(i >= n0)(lambda: emit(refs[1]))


def norm_mod(srcs, gain, mods, layer, which_shift, which_scale, tm=512):
    D = srcs[0].shape[1]
    M = sum(s.shape[0] for s in srcs)
    sh_idx = mods.index(layer, which_shift, tm)
    sc_idx = mods.index(layer, which_scale, tm)
    n0, src_specs = _row_sources(srcs, tm, D, lambda: 0)
    return pl.pallas_call(
        functools.partial(_norm_mod_body, n_src=len(srcs), n0=n0),
        grid=(M // tm,),
        in_specs=src_specs
        + [
            pl.BlockSpec((1, D), lambda i: (0, 0)),
            pl.BlockSpec((1, 1, D), lambda i: (sh_idx(i), 0, 0)),
            pl.BlockSpec((1, 1, D), lambda i: (sc_idx(i), 0, 0)),
        ],
        out_specs=pl.BlockSpec((tm, D), lambda i: (i, 0)),
        out_shape=jax.ShapeDtypeStruct((M, D), BF16),
        compiler_params=_params(1),
        name="norm_mod",
    )(*srcs, gain.reshape(1, D), mods.table, mods.table)


def _rms_body(x_ref, g_ref, o_ref):
    o_ref[...] = _rms(x_ref[...], g_ref[...]).astype(o_ref.dtype)


def final_norm(t, gain, m_rows, tm=512):
    D = t.shape[1]
    return pl.pallas_call(
        _rms_body,
        grid=(m_rows // tm,),
        in_specs=[pl.BlockSpec((tm, D), lambda i: (i, 0)), pl.BlockSpec((1, D), lambda i: (0, 0))],
        out_specs=pl.BlockSpec((tm, D), lambda i: (i, 0)),
        out_shape=jax.ShapeDtypeStruct((m_rows, D), F32),
        compiler_params=_params(1),
        name="final_norm",
    )(t, gain.reshape(1, D))


def _mm_body(a_ref, w_ref, o_ref):
    o_ref[...] = jnp.dot(a_ref[...], w_ref[...].astype(BF16), preferred_element_type=F32).astype(o_ref.dtype)


def matmul(a, w, wl, m_rows, n_cols, out_dtype, tm=1024, tn=512, name="mm"):
    K = w.shape[1]
    return pl.pallas_call(
        _mm_body,
        grid=(m_rows // tm, n_cols // tn),
        in_specs=[
            pl.BlockSpec((tm, K), lambda i, j: (i, 0)),
            pl.BlockSpec((None, K, tn), lambda i, j: (wl, 0, j)),
        ],
        out_specs=pl.BlockSpec((tm, tn), lambda i, j: (i, j)),
        out_shape=jax.ShapeDtypeStruct((m_rows, n_cols), out_dtype),
        compiler_params=_params(2),
        name=name,
    )(a, w)


def _mm_res_body(*refs, n_a, n_res, n0):
    a_refs, w_refs = refs[:n_a], refs[n_a : 2 * n_a]
    r_refs = refs[2 * n_a : 2 * n_a + n_res]
    g_ref, o_ref = refs[2 * n_a + n_res :]
    acc = jnp.dot(a_refs[0][...], w_refs[0][...].astype(BF16), preferred_element_type=F32)
    for a_ref, w_ref in zip(a_refs[1:], w_refs[1:]):
        acc = acc + jnp.dot(a_ref[...], w_ref[...].astype(BF16), preferred_element_type=F32)
    o_ref[...] = _from_source(r_refs, n0, pl.program_id(0)) + g_ref[0] * acc


def matmul_residual(a_parts, w, wl, res_srcs, mods, layer, which_gate, m_rows, tm=1024, tn=512, name="mm_res"):
    K, N = w.shape[1:]
    kp = K // len(a_parts)
    g_idx = mods.index(layer, which_gate, tm)
    n0, res_specs = _row_sources(res_srcs, tm, tn, lambda j: j)
    a_specs = [pl.BlockSpec((tm, kp), lambda i, j: (i, 0)) for _ in a_parts]
    w_specs = [pl.BlockSpec((None, kp, tn), lambda i, j, p=p: (wl, p, j)) for p in range(len(a_parts))]
    return pl.pallas_call(
        functools.partial(_mm_res_body, n_a=len(a_parts), n_res=len(res_srcs), n0=n0),
        grid=(m_rows // tm, N // tn),
        in_specs=a_specs + w_specs + res_specs + [pl.BlockSpec((1, 1, tn), lambda i, j: (g_idx(i), 0, j))],
        out_specs=pl.BlockSpec((tm, tn), lambda i, j: (i, j)),
        out_shape=jax.ShapeDtypeStruct((m_rows, N), F32),
        compiler_params=_params(2),
        name=name,
    )(*a_parts, *([w] * len(a_parts)), *res_srcs, mods.table)


def _swiglu_body(a_ref, wg_ref, wu_ref, wd_ref, o_ref, wd_bf_ref):
    a = a_ref[...]
    g = jnp.dot(a, wg_ref[...].astype(BF16), preferred_element_type=F32)
    u = jnp.dot(a, wu_ref[...].astype(BF16), preferred_element_type=F32)
    o_ref[...] = (_silu(g) * u).astype(o_ref.dtype)
    @pl.when(pl.program_id(0) == 0)
    def _():
        wd_bf_ref[...] = wd_ref[...].astype(BF16)


def matmul_swiglu(a, w_gu, w_down, wl, m_rows, tm=1024, tn=256):
    K, N2 = w_gu.shape[1:]
    H, Nd = w_down.shape[1:]
    assert N2 == 2 * H
    nj = H // tn
    wd_block = lambda i, j: jnp.where(i == 0, j, nj - 1)
    return pl.pallas_call(
        _swiglu_body,
        grid=(m_rows // tm, nj),
        in_specs=[
            pl.BlockSpec((tm, K), lambda i, j: (i, 0)),
            pl.BlockSpec((None, K, tn), lambda i, j: (wl, 0, j)),
            pl.BlockSpec((None, K, tn), lambda i, j: (wl, 0, j + nj)),
            pl.BlockSpec((None, tn, Nd), lambda i, j: (wl, wd_block(i, j), 0)),
        ],
        out_specs=[
            pl.BlockSpec((tm, tn), lambda i, j: (i, j)),
            pl.BlockSpec((None, tn, Nd), lambda i, j: (0, wd_block(i, j), 0)),
        ],
        out_shape=[jax.ShapeDtypeStruct((m_rows, H), BF16), jax.ShapeDtypeStruct((1, H, Nd), BF16)],
        compiler_params=_params(2),
        name="ffn_up",
    )(a, w_gu, w_gu, w_down)


def _pool_body(z_ref, w_ref, s_ref, o_ref, *, n_batch, ctx_len):
    b = pl.program_id(0)
    g = pl.program_id(1)
    rows, C = z_ref.shape

    def run(sub_len):
        n_sub = rows // sub_len
        stride = sub_len + POOL_HALO
        z = z_ref[...]
        gap = jnp.zeros((POOL_HALO, C), F32)
        zp = jnp.concatenate([p for k in range(n_sub) for p in (z[k * sub_len : (k + 1) * sub_len], gap)], axis=0)
        total = n_sub * stride
        t1 = lax.broadcasted_iota(jnp.int32, (sub_len, C), 0)
        t = t1 if n_sub == 1 else jnp.concatenate([t1] * n_sub, axis=0)

        def prev(a, k):
            return pltpu.roll(a, k, 0)

        def nxt(a, k):
            return pltpu.roll(a, total - k, 0)

        def window_sum(level):
            s = zp + prev(zp, 1)
            for lv in range(level):
                s = prev(s, 2**lv) + nxt(s, 2**lv)
            if n_sub == 1:
                return s[:sub_len]
            return jnp.concatenate([s[k * stride : k * stride + sub_len] for k in range(n_sub)], axis=0)

        for gi, window in enumerate(POOL_WINDOWS):

            @pl.when(g == gi)
            def _(gi=gi, window=window):
                half = window // 2
                cnt = (jnp.minimum(t - half + window, sub_len) - jnp.maximum(t - half, 0)).astype(F32)
                pooled = window_sum(gi) / cnt - z
                mixed = jnp.dot(pooled.astype(BF16), w_ref[0].astype(BF16), preferred_element_type=F32)
                o_ref[...] = (mixed * s_ref[...]).astype(o_ref.dtype)

    pl.when(b < n_batch)(lambda: run(rows))
    pl.when(b >= n_batch)(lambda: run(ctx_len))


def pool_mixer(z, pool_w, pool_scale, n_batch, seq, ctx_len):
    M = z.shape[0]
    n_groups, G, _ = pool_w.shape
    assert (M - n_batch * seq) % seq == 0 and seq % ctx_len == 0
    return pl.pallas_call(
        functools.partial(_pool_body, n_batch=n_batch, ctx_len=ctx_len),
        grid=(M // seq, n_groups),
        in_specs=[
            pl.BlockSpec((seq, G), lambda b, g: (b, g)),
            pl.BlockSpec((1, G, G), lambda b, g: (g, 0, 0)),
            pl.BlockSpec((1, G), lambda b, g: (0, g)),
        ],
        out_specs=pl.BlockSpec((seq, G), lambda b, g: (b, g)),
        out_shape=jax.ShapeDtypeStruct((M, n_groups * G), BF16),
        compiler_params=_params(2),
        name="pool_mixer",
    )(z, pool_w, pool_scale.reshape(1, n_groups * G))


def _sgu_body(u_ref, v_ref, ng_ref, ws_ref, bs_ref, o_ref):
    ws = ws_ref[0].astype(BF16)
    bs = bs_ref[0]
    ng = ng_ref[0]
    for c in range(u_ref.shape[0] // SGU_CHUNK):
        rows = pl.ds(c * SGU_CHUNK, SGU_CHUNK)
        vn = _rms(_gelu_tanh(v_ref[rows, :]), ng)
        gate = jnp.dot(ws, vn.astype(BF16), preferred_element_type=F32) + bs
        o_ref[rows, :] = (_gelu_tanh(u_ref[rows, :]) * gate).astype(o_ref.dtype)


def sgu_mixer(z, norm_g, w_s, b_s, col0, ts=2048):
    M = z.shape[0]
    H, P, _ = w_s.shape
    hd = norm_g.shape[1]
    cb0 = col0 // hd
    return pl.pallas_call(
        _sgu_body,
        grid=(M // ts, H),
        in_specs=[
            pl.BlockSpec((ts, hd), lambda i, h: (i, cb0 + h)),
            pl.BlockSpec((ts, hd), lambda i, h: (i, cb0 + H + h)),
            pl.BlockSpec((1, 1, hd), lambda i, h: (h, 0, 0)),
            pl.BlockSpec((1, P, P), lambda i, h: (h, 0, 0)),
            pl.BlockSpec((1, P, 1), lambda i, h: (h, 0, 0)),
        ],
        out_specs=pl.BlockSpec((ts, hd), lambda i, h: (i, h)),
        out_shape=jax.ShapeDtypeStruct((M, H * hd), BF16),
        compiler_params=_params(2),
        name="sgu_mixer",
    )(z, z, norm_g.reshape(H, 1, hd), w_s, b_s.reshape(H, P, 1))


def _rope_half(x, cos, sin_signed):
    return x * cos + pltpu.roll(x, LANE // 2, 1) * sin_signed


def _prep_q_body(zq_ref, qag_ref, qng_ref, cos_ref, sin_ref, cq_ref, gq_ref):
    cq_ref[...] = _rms(zq_ref[:, :MLA_Q_RANK], qag_ref[...]).astype(cq_ref.dtype)
    qs = _exp2_scale(1.0 / math.sqrt(GQA_HEAD_DIM))
    cos = cos_ref[...] * qs
    sin = sin_ref[...] * qs
    g = qng_ref[...]
    for h in range(GQA_HEADS):
        x = _rms(zq_ref[:, MLA_Q_RANK + h * LANE : MLA_Q_RANK + (h + 1) * LANE], g)
        gq_ref[:, h * LANE : (h + 1) * LANE] = _rope_half(x, cos, sin).astype(gq_ref.dtype)


def prep_q(zq, q_a_g, q_norm_g, cos_g, sin_g, rope_block):
    M, N = zq.shape
    tm = ROPE_TILE
    nq = GQA_HEADS * GQA_HEAD_DIM
    return pl.pallas_call(
        _prep_q_body,
        grid=(M // tm,),
        in_specs=[
            pl.BlockSpec((tm, N), lambda i: (i, 0)),
            pl.BlockSpec((1, MLA_Q_RANK), lambda i: (0, 0)),
            pl.BlockSpec((1, LANE), lambda i: (0, 0)),
            pl.BlockSpec((tm, LANE), rope_block),
            pl.BlockSpec((tm, LANE), rope_block),
        ],
        out_specs=[pl.BlockSpec((tm, MLA_Q_RANK), lambda i: (i, 0)), pl.BlockSpec((tm, nq), lambda i: (i, 0))],
        out_shape=[jax.ShapeDtypeStruct((M, MLA_Q_RANK), BF16), jax.ShapeDtypeStruct((M, nq), BF16)],
        compiler_params=_params(1),
        name="prep_q",
    )(zq, q_a_g.reshape(1, -1), q_norm_g.reshape(1, -1), cos_g, sin_g)


def _prep_kv_body(z_ref, kag_ref, kng_ref, cg_ref, sg_ref, cm_ref, sm_ref, ckv_ref, kr_ref, gk_ref, gv_ref):
    ckv_ref[...] = _rms(z_ref[:, :MLA_KV_RANK], kag_ref[...]).astype(ckv_ref.dtype)
    kr = z_ref[:, MLA_KV_RANK : MLA_KV_RANK + LANE]
    lane = lax.broadcasted_iota(jnp.int32, kr.shape, 1)
    half = MLA_ROPE // 2
    swapped = jnp.where(lane < half, pltpu.roll(kr, LANE - half, 1), pltpu.roll(kr, half, 1))
    kr_ref[...] = (kr * cm_ref[...] + swapped * sm_ref[...]).astype(kr_ref.dtype)
    off_k = MLA_KV_RANK + LANE
    off_v = off_k + GQA_KV_HEADS * LANE
    cos = cg_ref[...]
    sin = sg_ref[...]
    g = kng_ref[...]
    for h in range(GQA_KV_HEADS):
        x = _rms(z_ref[:, off_k + h * LANE : off_k + (h + 1) * LANE], g)
        gk_ref[:, h * LANE : (h + 1) * LANE] = _rope_half(x, cos, sin).astype(gk_ref.dtype)
    gv_ref[...] = z_ref[:, off_v : off_v + GQA_KV_HEADS * LANE].astype(gv_ref.dtype)


def prep_kv(zkv, kv_a_g, k_norm_g, cos_g, sin_g, cos_m, sin_m, rope_block):
    M, N = zkv.shape
    tm = ROPE_TILE
    nk = GQA_KV_HEADS * GQA_HEAD_DIM
    tok = lambda i: (i, 0)
    fixed = lambda i: (0, 0)
    return pl.pallas_call(
        _prep_kv_body,
        grid=(M // tm,),
        in_specs=[
            pl.BlockSpec((tm, N), tok),
            pl.BlockSpec((1, MLA_KV_RANK), fixed),
            pl.BlockSpec((1, LANE), fixed),
            pl.BlockSpec((tm, LANE), rope_block),
            pl.BlockSpec((tm, LANE), rope_block),
            pl.BlockSpec((tm, LANE), rope_block),
            pl.BlockSpec((tm, LANE), rope_block),
        ],
        out_specs=[
            pl.BlockSpec((tm, MLA_KV_RANK), tok),
            pl.BlockSpec((tm, LANE), tok),
            pl.BlockSpec((tm, nk), tok),
            pl.BlockSpec((tm, nk), tok),
        ],
        out_shape=[
            jax.ShapeDtypeStruct((M, MLA_KV_RANK), BF16),
            jax.ShapeDtypeStruct((M, LANE), BF16),
            jax.ShapeDtypeStruct((M, nk), BF16),
            jax.ShapeDtypeStruct((M, nk), BF16),
        ],
        compiler_params=_params(1),
        name="prep_kv",
    )(zkv, kv_a_g.reshape(1, -1), k_norm_g.reshape(1, -1), cos_g, sin_g, cos_m, sin_m)


def _mm_qrope_body(a_ref, w_ref, c_ref, s_ref, o_ref):
    acc = jnp.dot(a_ref[...], w_ref[...], preferred_element_type=F32) * _exp2_scale(
        1.0 / math.sqrt(MLA_NOPE + MLA_ROPE)
    )
    cos = c_ref[...]
    sin = s_ref[...]
    for h in range(acc.shape[1] // (2 * LANE)):
        lo = h * 2 * LANE
        o_ref[:, lo : lo + LANE] = acc[:, lo : lo + LANE].astype(o_ref.dtype)
        t = acc[:, lo + LANE : lo + 2 * LANE]
        o_ref[:, lo + LANE : lo + 2 * LANE] = (t * cos + pltpu.roll(t, LANE // 2, 1) * sin).astype(o_ref.dtype)


def matmul_qrope(a, w, cos_m, sin_m_abs, rope_block):
    M, K = a.shape
    tm = ROPE_TILE
    N = tn = w.shape[1]
    return pl.pallas_call(
        _mm_qrope_body,
        grid=(M // tm, N // tn),
        in_specs=[
            pl.BlockSpec((tm, K), lambda i, j: (i, 0)),
            pl.BlockSpec((K, tn), lambda i, j: (0, j)),
            pl.BlockSpec((tm, LANE), rope_block),
            pl.BlockSpec((tm, LANE), rope_block),
        ],
        out_specs=pl.BlockSpec((tm, tn), lambda i, j: (i, j)),
        out_shape=jax.ShapeDtypeStruct((M, N), BF16),
        compiler_params=_params(2),
        name="mla_q_up",
    )(a, w, cos_m, sin_m_abs)


_NT = (((1,), (1,)), ((), ()))
ATTN_SUB = 256


def _exp2_scale(scale):
    return scale * math.log2(math.e)


def _attend(q_ref, o_ref, s_ref, heads):
    n_sub = q_ref.shape[0] // ATTN_SUB
    chains = [(head, r) for head in heads for r in range(n_sub)]

    def scores(idx):
        (q_cols, _, kc, kx, _, _), r = chains[idx]
        slot = idx % s_ref.shape[0]
        n_c = kc.shape[0]
        q = q_ref[pl.ds(r * ATTN_SUB, ATTN_SUB), q_cols]
        s_ref[slot, :, :n_c] = lax.dot_general(q, kc, _NT, preferred_element_type=F32)
        s_ref[slot, :, n_c:] = lax.dot_general(q, kx, _NT, preferred_element_type=F32)
        s = s_ref[slot]
        return s, jnp.max(s, axis=-1, keepdims=True)

    nxt = scores(0)
    for idx, ((_, o_cols, kc, _, vc, vx), r) in enumerate(chains):
        n_c = kc.shape[0]
        s, m = nxt
        p = jnp.exp2(s - m)
        denom = jnp.sum(p, axis=-1, keepdims=True)
        p_c = p[:, :n_c]
        if idx + 1 < len(chains):
            nxt = scores(idx + 1)
            p_c = p_c + _zero_after(nxt[1])
        o = jnp.dot(p_c.astype(BF16), vc, preferred_element_type=F32)
        o = o + jnp.dot(p[:, n_c:].astype(BF16), vx, preferred_element_type=F32)
        o_ref[pl.ds(r * ATTN_SUB, ATTN_SUB), o_cols] = (o * (1.0 / denom)).astype(o_ref.dtype)


def _zero_after(x):
    bits = lax.bitcast_convert_type(x, jnp.uint32)
    return lax.bitcast_convert_type((bits >> 16) >> 16, F32)


def _score_scratch(n_keys):
    return [pltpu.VMEM((2, ATTN_SUB, n_keys), F32)]


MLA_STEP_HEADS = 1


def _mla_body(q_ref, kvc_ref, kvx_ref, krc_ref, krx_ref, o_ref, s_ref):
    krc = krc_ref[...]
    krx = krx_ref[...]
    heads = []
    for h in range(MLA_STEP_HEADS):
        kn = slice(2 * h * LANE, (2 * h + 1) * LANE)
        v = slice((2 * h + 1) * LANE, (2 * h + 2) * LANE)
        heads.append((
            slice(2 * h * LANE, (2 * h + 2) * LANE),
            slice(h * LANE, (h + 1) * LANE),
            jnp.concatenate([kvc_ref[:, kn], krc], axis=1),
            jnp.concatenate([kvx_ref[:, kn], krx], axis=1),
            kvc_ref[:, v],
            kvx_ref[:, v],
        ))
    _attend(q_ref, o_ref, s_ref, heads)


def mla_attention(q, kv, kr, n_batch, seq, ctx_len):
    cb0 = n_batch * seq // ctx_len
    hw = MLA_STEP_HEADS * 2 * LANE
    return pl.pallas_call(
        _mla_body,
        grid=(n_batch, MLA_HEADS // MLA_STEP_HEADS),
        in_specs=[
            pl.BlockSpec((seq, hw), lambda b, g: (b, g)),
            pl.BlockSpec((ctx_len, hw), lambda b, g: (cb0 + b, g)),
            pl.BlockSpec((seq, hw), lambda b, g: (b, g)),
            pl.BlockSpec((ctx_len, LANE), lambda b, g: (cb0 + b, 0)),
            pl.BlockSpec((seq, LANE), lambda b, g: (b, 0)),
        ],
        out_specs=pl.BlockSpec((seq, MLA_STEP_HEADS * LANE), lambda b, g: (b, g)),
        out_shape=jax.ShapeDtypeStruct((n_batch * seq, MLA_HEADS * MLA_V), BF16),
        scratch_shapes=_score_scratch(ctx_len + seq),
        compiler_params=_params(2),
        name="mla_attention",
    )(q, kv, kv, kr, kr)


def _gqa_body(q_ref, kc_ref, kx_ref, vc_ref, vx_ref, o_ref, s_ref):
    kc, kx, vc, vx = kc_ref[...], kx_ref[...], vc_ref[...], vx_ref[...]
    cols = [slice(h * LANE, (h + 1) * LANE) for h in range(q_ref.shape[1] // LANE)]
    _attend(q_ref, o_ref, s_ref, [(c, c, kc, kx, vc, vx) for c in cols])


GQA_STEP_HEADS = 1


def gqa_attention(q, k, v, n_batch, seq, ctx_len):
    cb0 = n_batch * seq // ctx_len
    qw = GQA_STEP_HEADS * LANE
    per_kv = GQA_HEADS // GQA_KV_HEADS // GQA_STEP_HEADS
    return pl.pallas_call(
        _gqa_body,
        grid=(n_batch, GQA_HEADS // GQA_STEP_HEADS),
        in_specs=[
            pl.BlockSpec((seq, qw), lambda b, g: (b, g)),
            pl.BlockSpec((ctx_len, LANE), lambda b, g: (cb0 + b, g // per_kv)),
            pl.BlockSpec((seq, LANE), lambda b, g: (b, g // per_kv)),
            pl.BlockSpec((ctx_len, LANE), lambda b, g: (cb0 + b, g // per_kv)),
            pl.BlockSpec((seq, LANE), lambda b, g: (b, g // per_kv)),
        ],
        out_specs=pl.BlockSpec((seq, qw), lambda b, g: (b, g)),
        out_shape=jax.ShapeDtypeStruct((n_batch * seq, GQA_HEADS * GQA_HEAD_DIM), BF16),
        scratch_shapes=_score_scratch(ctx_len + seq),
        compiler_params=_params(2),
        name="gqa_attention",
    )(q, k, k, v, v)


ROPE_TILE = 512


def _rope_tables(seq):
    rows = seq // GRID_W
    row = np.repeat(np.arange(rows, dtype=np.float64), GRID_W)
    col = np.tile(np.arange(GRID_W, dtype=np.float64), rows)

    def cos_sin(rot_dim):
        n_freq = rot_dim // 4
        inv = ROPE_THETA ** (-np.arange(n_freq, dtype=np.float64) / n_freq)
        ang = np.concatenate([row[:, None] * inv, col[:, None] * inv], axis=-1)
        return np.cos(ang), np.sin(ang)

    def table(x_part, ident_row):
        ident = np.broadcast_to(ident_row, (ROPE_TILE, LANE))
        return jnp.asarray(np.concatenate([x_part, ident], axis=0), F32)

    ones, zeros = np.ones((1, LANE)), np.zeros((1, LANE))
    cg, sg = cos_sin(GQA_HEAD_DIM)
    cos_g = table(np.concatenate([cg, cg], axis=1), ones)
    sin_g = table(np.concatenate([-sg, sg], axis=1), zeros)
    cm, sm = cos_sin(MLA_ROPE)
    zpad = np.zeros((seq, LANE - MLA_ROPE))
    ident_m = np.concatenate([np.ones((1, MLA_ROPE)), np.zeros((1, LANE - MLA_ROPE))], axis=1)
    cos_m = table(np.concatenate([cm, cm, zpad], axis=1), ident_m)
    sin_m = table(np.concatenate([-sm, sm, zpad], axis=1), zeros)
    sin_m_abs = table(np.concatenate([sm, sm, zpad], axis=1), zeros)
    return cos_g, sin_g, cos_m, sin_m, sin_m_abs


def _rope_block(seq, n_latent_rows):
    per_seq = seq // ROPE_TILE
    n_latent = n_latent_rows // ROPE_TILE
    return lambda i, *_: (jnp.where(i < n_latent, i % per_seq, per_seq), 0)


def _mla_q_weight(w_q_b):
    r = w_q_b.shape[0]
    w = w_q_b.reshape(r, MLA_HEADS, MLA_NOPE + MLA_ROPE)
    nope, rope = w[..., :MLA_NOPE], w[..., MLA_NOPE:]
    x1, x2 = rope[..., : MLA_ROPE // 2], rope[..., MLA_ROPE // 2 :]
    return jnp.concatenate([nope, rope, -x2, x1], axis=-1).reshape(r, MLA_HEADS * 2 * LANE).astype(BF16)


def _odd_kv_weight(w_in, q_cols):
    w = w_in[:, q_cols:]
    pad = jnp.zeros((w.shape[0], LANE - MLA_ROPE), w.dtype)
    split = MLA_KV_RANK + MLA_ROPE
    return jnp.concatenate([w[:, :split], pad, w[:, split:]], axis=1).astype(BF16)[None]


def kernel(x, c, ctx, c_ctx, mod_w, mod_b, norm1_g, norm2_g, final_g, ffn_w_gu, ffn_w_down, ev_w_in, ev_pool_w,
           ev_pool_scale, ev_sgu_norm_g, ev_sgu_w_s, ev_sgu_b, ev_w_out, od_w_in, od_q_a_g, od_w_q_b, od_kv_a_g,
           od_w_kv_b, od_q_norm_g, od_k_norm_g, od_w_out):
    B, S, D = x.shape
    CL = ctx.shape[1]
    depth = mod_w.shape[0]
    assert depth == 2, "layer schedule below is written for one even and one odd layer"
    n_x = B * S
    n_all = n_x + B * CL
    x_rows = x.reshape(n_x, D)
    c_rows = ctx.reshape(B * CL, D)

    cc = jnp.concatenate([c, c_ctx[None, :], jnp.zeros((MOD_ROWS - B - 1, D), F32)], axis=0)
    mods = Mods(adaln_all(cc, mod_w, mod_b), B, S)

    def ffn(t_in, layer, m_rows):
        h = norm_mod([t_in], norm2_g[layer], mods, layer, SH2, SC2)
        act, w_down = matmul_swiglu(h, ffn_w_gu, ffn_w_down, layer, m_rows)
        return matmul_residual([act], w_down, 0, [t_in], mods, layer, G2, m_rows, tm=512, tn=512,
                               name="ffn_down")

    pool_width = ev_pool_scale.shape[1]
    h = norm_mod([x_rows, c_rows], norm1_g[0], mods, 0, SH1, SC1)
    z = matmul(h, ev_w_in, 0, n_all, ev_w_in.shape[2], F32, name="even_in")
    ya = pool_mixer(z, ev_pool_w[0], ev_pool_scale[0], B, S, CL)
    yb = sgu_mixer(z, ev_sgu_norm_g[0], ev_sgu_w_s[0], ev_sgu_b[0], pool_width)
    t = matmul_residual([ya, yb], ev_w_out.astype(BF16), 0, [x_rows, c_rows], mods, 0, G1, n_all,
                        name="even_out")
    t = ffn(t, 0, n_all)

    q_cols = MLA_Q_RANK + GQA_HEADS * GQA_HEAD_DIM
    cos_g, sin_g, cos_m, sin_m, sin_m_abs = _rope_tables(S)
    rope_block = _rope_block(S, n_x)
    h = norm_mod([t], norm1_g[1], mods, 1, SH1, SC1)
    w_in_bf = od_w_in[0].astype(BF16)
    zq = matmul(h, w_in_bf[None, :, :q_cols], 0, n_x, q_cols, F32, name="odd_in_q")
    w_kv = _odd_kv_weight(w_in_bf, q_cols)
    zkv = matmul(h, w_kv, 0, n_all, w_kv.shape[2], F32, tm=512, tn=w_kv.shape[2], name="odd_in_kv")
    cq, gq = prep_q(zq, od_q_a_g[0], od_q_norm_g[0], cos_g, sin_g, rope_block)
    ckv, kr, gk, gv = prep_kv(zkv, od_kv_a_g[0], od_k_norm_g[0], cos_g, sin_g, cos_m, sin_m, rope_block)
    q_m = matmul_qrope(cq, _mla_q_weight(od_w_q_b[0]), cos_m, sin_m_abs, rope_block)
    kv_m = matmul(ckv, od_w_kv_b, 0, n_all, od_w_kv_b.shape[2], BF16, tm=512, tn=od_w_kv_b.shape[2],
                  name="mla_kv_up")
    o_m = mla_attention(q_m, kv_m, kr, B, S, CL)
    o_g = gqa_attention(gq, gk, gv, B, S, CL)
    t = matmul_residual([o_m, o_g], od_w_out, 0, [t], mods, 1, G1, n_x, name="odd_out")
    t = ffn(t, 1, n_x)

    return final_norm(t, final_g, n_x).reshape(B, S, D)
```

```python
import functools
import math

import jax
import jax.numpy as jnp
import numpy as np
from jax import lax
from jax.experimental import pallas as pl
from jax.experimental.pallas import tpu as pltpu

F32 = jnp.float32
BF16 = jnp.bfloat16

GRID_W = 64
EPS = 1e-6
ROPE_THETA = 10000.0
POOL_WINDOWS = (2, 4, 8, 16)
POOL_HALO = 16
SGU_CHUNK = 128
MLA_HEADS = 16
MLA_Q_RANK = 1024
MLA_KV_RANK = 512
MLA_NOPE = 128
MLA_ROPE = 64
MLA_V = 128
GQA_HEADS = 16
GQA_KV_HEADS = 4
GQA_HEAD_DIM = 128
LANE = 128
MOD_ROWS = 16
SH1, SC1, G1, SH2, SC2, G2 = range(6)

VMEM_LIMIT = 56 * 1024 * 1024


def _params(n_axes):
    return pltpu.CompilerParams(dimension_semantics=("arbitrary",) * n_axes, vmem_limit_bytes=VMEM_LIMIT)


def _silu(x):
    return x / (1.0 + jnp.exp(-x))


def _gelu_tanh(x):
    return 0.5 * x * (1.0 + jnp.tanh(math.sqrt(2.0 / math.pi) * (x + 0.044715 * (x * x * x))))


def _rms(x, g):
    return x * lax.rsqrt(jnp.mean(x * x, axis=-1, keepdims=True) + EPS) * g


def _row_sources(srcs, tm, width, col_of):
    n0 = srcs[0].shape[0] // tm
    if len(srcs) == 1:
        return n0, [pl.BlockSpec((tm, width), lambda i, *r: (i, col_of(*r)))]
    n1 = srcs[1].shape[0] // tm
    return n0, [
        pl.BlockSpec((tm, width), lambda i, *r: (jnp.minimum(i, n0 - 1), col_of(*r))),
        pl.BlockSpec((tm, width), lambda i, *r: (jnp.clip(i - n0, 0, n1 - 1), col_of(*r))),
    ]


def _from_source(refs, n0, i):
    if len(refs) == 1:
        return refs[0][...]
    return jnp.where(i < n0, refs[0][...], refs[1][...])


def _adaln_body(c_ref, w_ref, b_ref, o_ref):
    sc = _silu(c_ref[...]).astype(BF16)
    m = jnp.dot(sc, w_ref[0].astype(BF16), preferred_element_type=F32) + b_ref[0]
    for r in range(MOD_ROWS):
        o_ref[r] = m[r : r + 1]


def adaln_all(cc, mod_w, mod_b, tn=1024):
    L, D, N = mod_w.shape
    per_chunk = D // tn
    return pl.pallas_call(
        _adaln_body,
        grid=(L, N // tn),
        in_specs=[
            pl.BlockSpec((MOD_ROWS, D), lambda l, j: (0, 0)),
            pl.BlockSpec((1, D, tn), lambda l, j: (l, 0, j)),
            pl.BlockSpec((1, 1, tn), lambda l, j: (l, 0, j)),
        ],
        out_specs=pl.BlockSpec((MOD_ROWS, 1, tn), lambda l, j: (l * (N // D) + j // per_chunk, 0, j % per_chunk)),
        out_shape=jax.ShapeDtypeStruct((L * (N // D) * MOD_ROWS, 1, D), F32),
        compiler_params=_params(2),
        name="adaln",
    )(cc, mod_w, mod_b.reshape(L, 1, N))


class Mods:
    def __init__(self, table, n_batch, seq):
        self.table = table
        self.n_batch = n_batch
        self.seq = seq

    def index(self, layer, which, tm):
        n_x = self.n_batch * self.seq // tm
        base = (layer * 6 + which) * MOD_ROWS
        return lambda i: base + jnp.where(i < n_x, (i * tm) // self.seq, self.n_batch)


def _norm_mod_body(*refs, n_src, n0):
    g_ref, sh_ref, sc_ref, o_ref = refs[n_src:]

    def emit(x_ref):
        o_ref[...] = (_rms(x_ref[...], g_ref[...] * (1.0 + sc_ref[0])) + sh_ref[0]).astype(o_ref.dtype)

    if n_src == 1:
        emit(refs[0])
    else:
        i = pl.program_id(0)
        pl.when(i < n0)(lambda: emit(refs[0]))
        pl.when(i >= n0)(lambda: emit(refs[1]))


def norm_mod(srcs, gain, mods, layer, which_shift, which_scale, tm=512):
    D = srcs[0].shape[1]
    M = sum(s.shape[0] for s in srcs)
    sh_idx = mods.index(layer, which_shift, tm)
    sc_idx = mods.index(layer, which_scale, tm)
    n0, src_specs = _row_sources(srcs, tm, D, lambda: 0)
    return pl.pallas_call(
        functools.partial(_norm_mod_body, n_src=len(srcs), n0=n0),
        grid=(M // tm,),
        in_specs=src_specs
        + [
            pl.BlockSpec((1, D), lambda i: (0, 0)),
            pl.BlockSpec((1, 1, D), lambda i: (sh_idx(i), 0, 0)),
            pl.BlockSpec((1, 1, D), lambda i: (sc_idx(i), 0, 0)),
        ],
        out_specs=pl.BlockSpec((tm, D), lambda i: (i, 0)),
        out_shape=jax.ShapeDtypeStruct((M, D), BF16),
        compiler_params=_params(1),
        name="norm_mod",
    )(*srcs, gain.reshape(1, D), mods.table, mods.table)


def _rms_body(x_ref, g_ref, o_ref):
    o_ref[...] = _rms(x_ref[...], g_ref[...]).astype(o_ref.dtype)


def final_norm(t, gain, m_rows, tm=512):
    D = t.shape[1]
    return pl.pallas_call(
        _rms_body,
        grid=(m_rows // tm,),
        in_specs=[pl.BlockSpec((tm, D), lambda i: (i, 0)), pl.BlockSpec((1, D), lambda i: (0, 0))],
        out_specs=pl.BlockSpec((tm, D), lambda i: (i, 0)),
        out_shape=jax.ShapeDtypeStruct((m_rows, D), F32),
        compiler_params=_params(1),
        name="final_norm",
    )(t, gain.reshape(1, D))


def _mm_body(a_ref, w_ref, o_ref):
    o_ref[...] = jnp.dot(a_ref[...], w_ref[...].astype(BF16), preferred_element_type=F32).astype(o_ref.dtype)


def matmul(a, w, wl, m_rows, n_cols, out_dtype, tm=1024, tn=512, name="mm"):
    K = w.shape[1]
    return pl.pallas_call(
        _mm_body,
        grid=(m_rows // tm, n_cols // tn),
        in_specs=[
            pl.BlockSpec((tm, K), lambda i, j: (i, 0)),
            pl.BlockSpec((None, K, tn), lambda i, j: (wl, 0, j)),
        ],
        out_specs=pl.BlockSpec((tm, tn), lambda i, j: (i, j)),
        out_shape=jax.ShapeDtypeStruct((m_rows, n_cols), out_dtype),
        compiler_params=_params(2),
        name=name,
    )(a, w)


def _mm_res_body(*refs, n_a, n_res, n0):
    a_refs, w_refs = refs[:n_a], refs[n_a : 2 * n_a]
    r_refs = refs[2 * n_a : 2 * n_a + n_res]
    g_ref, o_ref = refs[2 * n_a + n_res :]
    acc = jnp.dot(a_refs[0][...], w_refs[0][...].astype(BF16), preferred_element_type=F32)
    for a_ref, w_ref in zip(a_refs[1:], w_refs[1:]):
        acc = acc + jnp.dot(a_ref[...], w_ref[...].astype(BF16), preferred_element_type=F32)
    o_ref[...] = _from_source(r_refs, n0, pl.program_id(0)) + g_ref[0] * acc


def matmul_residual(a_parts, w, wl, res_srcs, mods, layer, which_gate, m_rows, tm=1024, tn=512, name="mm_res"):
    K, N = w.shape[1:]
    kp = K // len(a_parts)
    g_idx = mods.index(layer, which_gate, tm)
    n0, res_specs = _row_sources(res_srcs, tm, tn, lambda j: j)
    a_specs = [pl.BlockSpec((tm, kp), lambda i, j: (i, 0)) for _ in a_parts]
    w_specs = [pl.BlockSpec((None, kp, tn), lambda i, j, p=p: (wl, p, j)) for p in range(len(a_parts))]
    return pl.pallas_call(
        functools.partial(_mm_res_body, n_a=len(a_parts), n_res=len(res_srcs), n0=n0),
        grid=(m_rows // tm, N // tn),
        in_specs=a_specs + w_specs + res_specs + [pl.BlockSpec((1, 1, tn), lambda i, j: (g_idx(i), 0, j))],
        out_specs=pl.BlockSpec((tm, tn), lambda i, j: (i, j)),
        out_shape=jax.ShapeDtypeStruct((m_rows, N), F32),
        compiler_params=_params(2),
        name=name,
    )(*a_parts, *([w] * len(a_parts)), *res_srcs, mods.table)


def _swiglu_body(a_ref, wg_ref, wu_ref, wd_ref, o_ref, wd_bf_ref):
    a = a_ref[...]
    g = jnp.dot(a, wg_ref[...].astype(BF16), preferred_element_type=F32)
    u = jnp.dot(a, wu_ref[...].astype(BF16), preferred_element_type=F32)
    o_ref[...] = (_silu(g) * u).astype(o_ref.dtype)
    @pl.when(pl.program_id(0) == 0)
    def _():
        wd_bf_ref[...] = wd_ref[...].astype(BF16)


def matmul_swiglu(a, w_gu, w_down, wl, m_rows, tm=1024, tn=256):
    K, N2 = w_gu.shape[1:]
    H, Nd = w_down.shape[1:]
    assert N2 == 2 * H
    nj = H // tn
    wd_block = lambda i, j: jnp.where(i == 0, j, nj - 1)
    return pl.pallas_call(
        _swiglu_body,
        grid=(m_rows // tm, nj),
        in_specs=[
            pl.BlockSpec((tm, K), lambda i, j: (i, 0)),
            pl.BlockSpec((None, K, tn), lambda i, j: (wl, 0, j)),
            pl.BlockSpec((None, K, tn), lambda i, j: (wl, 0, j + nj)),
            pl.BlockSpec((None, tn, Nd), lambda i, j: (wl, wd_block(i, j), 0)),
        ],
        out_specs=[
            pl.BlockSpec((tm, tn), lambda i, j: (i, j)),
            pl.BlockSpec((None, tn, Nd), lambda i, j: (0, wd_block(i, j), 0)),
        ],
        out_shape=[jax.ShapeDtypeStruct((m_rows, H), BF16), jax.ShapeDtypeStruct((1, H, Nd), BF16)],
        compiler_params=_params(2),
        name="ffn_up",
    )(a, w_gu, w_gu, w_down)


def _pool_body(z_ref, w_ref, s_ref, o_ref, *, n_batch, ctx_len):
    b = pl.program_id(0)
    g = pl.program_id(1)
    rows, C = z_ref.shape

    def run(sub_len):
        n_sub = rows // sub_len
        stride = sub_len + POOL_HALO
        z = z_ref[...]
        gap = jnp.zeros((POOL_HALO, C), F32)
        zp = jnp.concatenate([p for k in range(n_sub) for p in (z[k * sub_len : (k + 1) * sub_len], gap)], axis=0)
        total = n_sub * stride
        t1 = lax.broadcasted_iota(jnp.int32, (sub_len, C), 0)
        t = t1 if n_sub == 1 else jnp.concatenate([t1] * n_sub, axis=0)

        def prev(a, k):
            return pltpu.roll(a, k, 0)

        def nxt(a, k):
            return pltpu.roll(a, total - k, 0)

        def window_sum(level):
            s = zp + prev(zp, 1)
            for lv in range(level):
                s = prev(s, 2**lv) + nxt(s, 2**lv)
            if n_sub == 1:
                return s[:sub_len]
            return jnp.concatenate([s[k * stride : k * stride + sub_len] for k in range(n_sub)], axis=0)

        for gi, window in enumerate(POOL_WINDOWS):

            @pl.when(g == gi)
            def _(gi=gi, window=window):
                half = window // 2
                cnt = (jnp.minimum(t - half + window, sub_len) - jnp.maximum(t - half, 0)).astype(F32)
                pooled = window_sum(gi) / cnt - z
                mixed = jnp.dot(pooled.astype(BF16), w_ref[0].astype(BF16), preferred_element_type=F32)
                o_ref[...] = (mixed * s_ref[...]).astype(o_ref.dtype)

    pl.when(b < n_batch)(lambda: run(rows))
    pl.when(b >= n_batch)(lambda: run(ctx_len))


def pool_mixer(z, pool_w, pool_scale, n_batch, seq, ctx_len):
    M = z.shape[0]
    n_groups, G, _ = pool_w.shape
    assert (M - n_batch * seq) % seq == 0 and seq % ctx_len == 0
    return pl.pallas_call(
        functools.partial(_pool_body, n_batch=n_batch, ctx_len=ctx_len),
        grid=(M // seq, n_groups),
        in_specs=[
            pl.BlockSpec((seq, G), lambda b, g: (b, g)),
            pl.BlockSpec((1, G, G), lambda b, g: (g, 0, 0)),
            pl.BlockSpec((1, G), lambda b, g: (0, g)),
        ],
        out_specs=pl.BlockSpec((seq, G), lambda b, g: (b, g)),
        out_shape=jax.ShapeDtypeStruct((M, n_groups * G), BF16),
        compiler_params=_params(2),
        name="pool_mixer",
    )(z, pool_w, pool_scale.reshape(1, n_groups * G))


def _sgu_body(u_ref, v_ref, ng_ref, ws_ref, bs_ref, o_ref):
    ws = ws_ref[0].astype(BF16)
    bs = bs_ref[0]
    ng = ng_ref[0]
    for c in range(u_ref.shape[0] // SGU_CHUNK):
        rows = pl.ds(c * SGU_CHUNK, SGU_CHUNK)
        vn = _rms(_gelu_tanh(v_ref[rows, :]), ng)
        gate = jnp.dot(ws, vn.astype(BF16), preferred_element_type=F32) + bs
        o_ref[rows, :] = (_gelu_tanh(u_ref[rows, :]) * gate).astype(o_ref.dtype)


def sgu_mixer(z, norm_g, w_s, b_s, col0, ts=2048):
    M = z.shape[0]
    H, P, _ = w_s.shape
    hd = norm_g.shape[1]
    cb0 = col0 // hd
    return pl.pallas_call(
        _sgu_body,
        grid=(M // ts, H),
        in_specs=[
            pl.BlockSpec((ts, hd), lambda i, h: (i, cb0 + h)),
            pl.BlockSpec((ts, hd), lambda i, h: (i, cb0 + H + h)),
            pl.BlockSpec((1, 1, hd), lambda i, h: (h, 0, 0)),
            pl.BlockSpec((1, P, P), lambda i, h: (h, 0, 0)),
            pl.BlockSpec((1, P, 1), lambda i, h: (h, 0, 0)),
        ],
        out_specs=pl.BlockSpec((ts, hd), lambda i, h: (i, h)),
        out_shape=jax.ShapeDtypeStruct((M, H * hd), BF16),
        compiler_params=_params(2),
        name="sgu_mixer",
    )(z, z, norm_g.reshape(H, 1, hd), w_s, b_s.reshape(H, P, 1))


def _rope_half(x, cos, sin_signed):
    return x * cos + pltpu.roll(x, LANE // 2, 1) * sin_signed


def _prep_q_body(zq_ref, qag_ref, qng_ref, cos_ref, sin_ref, cq_ref, gq_ref):
    cq_ref[...] = _rms(zq_ref[:, :MLA_Q_RANK], qag_ref[...]).astype(cq_ref.dtype)
    qs = _exp2_scale(1.0 / math.sqrt(GQA_HEAD_DIM))
    cos = cos_ref[...] * qs
    sin = sin_ref[...] * qs
    g = qng_ref[...]
    for h in range(GQA_HEADS):
        x = _rms(zq_ref[:, MLA_Q_RANK + h * LANE : MLA_Q_RANK + (h + 1) * LANE], g)
        gq_ref[:, h * LANE : (h + 1) * LANE] = _rope_half(x, cos, sin).astype(gq_ref.dtype)


def prep_q(zq, q_a_g, q_norm_g, cos_g, sin_g, rope_block):
    M, N = zq.shape
    tm = ROPE_TILE
    nq = GQA_HEADS * GQA_HEAD_DIM
    return pl.pallas_call(
        _prep_q_body,
        grid=(M // tm,),
        in_specs=[
            pl.BlockSpec((tm, N), lambda i: (i, 0)),
            pl.BlockSpec((1, MLA_Q_RANK), lambda i: (0, 0)),
            pl.BlockSpec((1, LANE), lambda i: (0, 0)),
            pl.BlockSpec((tm, LANE), rope_block),
            pl.BlockSpec((tm, LANE), rope_block),
        ],
        out_specs=[pl.BlockSpec((tm, MLA_Q_RANK), lambda i: (i, 0)), pl.BlockSpec((tm, nq), lambda i: (i, 0))],
        out_shape=[jax.ShapeDtypeStruct((M, MLA_Q_RANK), BF16), jax.ShapeDtypeStruct((M, nq), BF16)],
        compiler_params=_params(1),
        name="prep_q",
    )(zq, q_a_g.reshape(1, -1), q_norm_g.reshape(1, -1), cos_g, sin_g)


def _prep_kv_body(z_ref, kag_ref, kng_ref, cg_ref, sg_ref, cm_ref, sm_ref, ckv_ref, kr_ref, gk_ref, gv_ref):
    ckv_ref[...] = _rms(z_ref[:, :MLA_KV_RANK], kag_ref[...]).astype(ckv_ref.dtype)
    kr = z_ref[:, MLA_KV_RANK : MLA_KV_RANK + LANE]
    lane = lax.broadcasted_iota(jnp.int32, kr.shape, 1)
    half = MLA_ROPE // 2
    swapped = jnp.where(lane < half, pltpu.roll(kr, LANE - half, 1), pltpu.roll(kr, half, 1))
    kr_ref[...] = (kr * cm_ref[...] + swapped * sm_ref[...]).astype(kr_ref.dtype)
    off_k = MLA_KV_RANK + LANE
    off_v = off_k + GQA_KV_HEADS * LANE
    cos = cg_ref[...]
    sin = sg_ref[...]
    g = kng_ref[...]
    for h in range(GQA_KV_HEADS):
        x = _rms(z_ref[:, off_k + h * LANE : off_k + (h + 1) * LANE], g)
        gk_ref[:, h * LANE : (h + 1) * LANE] = _rope_half(x, cos, sin).astype(gk_ref.dtype)
    gv_ref[...] = z_ref[:, off_v : off_v + GQA_KV_HEADS * LANE].astype(gv_ref.dtype)


def prep_kv(zkv, kv_a_g, k_norm_g, cos_g, sin_g, cos_m, sin_m, rope_block):
    M, N = zkv.shape
    tm = ROPE_TILE
    nk = GQA_KV_HEADS * GQA_HEAD_DIM
    tok = lambda i: (i, 0)
    fixed = lambda i: (0, 0)
    return pl.pallas_call(
        _prep_kv_body,
        grid=(M // tm,),
        in_specs=[
            pl.BlockSpec((tm, N), tok),
            pl.BlockSpec((1, MLA_KV_RANK), fixed),
            pl.BlockSpec((1, LANE), fixed),
            pl.BlockSpec((tm, LANE), rope_block),
            pl.BlockSpec((tm, LANE), rope_block),
            pl.BlockSpec((tm, LANE), rope_block),
            pl.BlockSpec((tm, LANE), rope_block),
        ],
        out_specs=[
            pl.BlockSpec((tm, MLA_KV_RANK), tok),
            pl.BlockSpec((tm, LANE), tok),
            pl.BlockSpec((tm, nk), tok),
            pl.BlockSpec((tm, nk), tok),
        ],
        out_shape=[
            jax.ShapeDtypeStruct((M, MLA_KV_RANK), BF16),
            jax.ShapeDtypeStruct((M, LANE), BF16),
            jax.ShapeDtypeStruct((M, nk), BF16),
            jax.ShapeDtypeStruct((M, nk), BF16),
        ],
        compiler_params=_params(1),
        name="prep_kv",
    )(zkv, kv_a_g.reshape(1, -1), k_norm_g.reshape(1, -1), cos_g, sin_g, cos_m, sin_m)


def _mm_qrope_body(a_ref, w_ref, c_ref, s_ref, o_ref):
    acc = jnp.dot(a_ref[...], w_ref[...], preferred_element_type=F32) * _exp2_scale(
        1.0 / math.sqrt(MLA_NOPE + MLA_ROPE)
    )
    cos = c_ref[...]
    sin = s_ref[...]
    for h in range(acc.shape[1] // (2 * LANE)):
        lo = h * 2 * LANE
        o_ref[:, lo : lo + LANE] = acc[:, lo : lo + LANE].astype(o_ref.dtype)
        t = acc[:, lo + LANE : lo + 2 * LANE]
        o_ref[:, lo + LANE : lo + 2 * LANE] = (t * cos + pltpu.roll(t, LANE // 2, 1) * sin).astype(o_ref.dtype)


def matmul_qrope(a, w, cos_m, sin_m_abs, rope_block):
    M, K = a.shape
    tm = ROPE_TILE
    N = tn = w.shape[1]
    return pl.pallas_call(
        _mm_qrope_body,
        grid=(M // tm, N // tn),
        in_specs=[
            pl.BlockSpec((tm, K), lambda i, j: (i, 0)),
            pl.BlockSpec((K, tn), lambda i, j: (0, j)),
            pl.BlockSpec((tm, LANE), rope_block),
            pl.BlockSpec((tm, LANE), rope_block),
        ],
        out_specs=pl.BlockSpec((tm, tn), lambda i, j: (i, j)),
        out_shape=jax.ShapeDtypeStruct((M, N), BF16),
        compiler_params=_params(2),
        name="mla_q_up",
    )(a, w, cos_m, sin_m_abs)


_NT = (((1,), (1,)), ((), ()))
ATTN_SUB = 1024


def _exp2_scale(scale):
    return scale * math.log2(math.e)


def _attend(q_ref, o_ref, s_ref, heads):
    n_sub = q_ref.shape[0] // ATTN_SUB
    heads = [(qc, oc, kc, kx, _with_ones(vc), _with_ones(vx)) for qc, oc, kc, kx, vc, vx in heads]
    chains = [(head, r) for head in heads for r in range(n_sub)]

    def scores(idx):
        (q_cols, _, kc, kx, _, _), r = chains[idx]
        slot = idx % s_ref.shape[0]
        n_c = kc.shape[0]
        q = q_ref[pl.ds(r * ATTN_SUB, ATTN_SUB), q_cols]
        s_ref[slot, :, :n_c] = lax.dot_general(q, kc, _NT, preferred_element_type=F32)
        s_ref[slot, :, n_c:] = lax.dot_general(q, kx, _NT, preferred_element_type=F32)
        s = s_ref[slot]
        return s, jnp.max(s, axis=-1, keepdims=True)

    nxt = scores(0)
    for idx, ((_, o_cols, kc, _, vc, vx), r) in enumerate(chains):
        n_c = kc.shape[0]
        s, m = nxt
        p = jnp.exp2(s - m).astype(BF16)
        if idx + 1 < len(chains):
            nxt = scores(idx + 1)
        o = jnp.dot(p[:, :n_c], vc, preferred_element_type=F32)
        o = o + jnp.dot(p[:, n_c:], vx, preferred_element_type=F32)
        d_v = o.shape[1] - LANE
        inv = 1.0 / o[:, d_v : d_v + 1]
        o_ref[pl.ds(r * ATTN_SUB, ATTN_SUB), o_cols] = (o[:, :d_v] * inv).astype(o_ref.dtype)


def _with_ones(v):
    return jnp.concatenate([v, jnp.ones((v.shape[0], LANE), v.dtype)], axis=1)


def _score_scratch(n_keys):
    return [pltpu.VMEM((2, ATTN_SUB, n_keys), F32)]


MLA_STEP_HEADS = 1


def _mla_body(q_ref, kvc_ref, kvx_ref, krc_ref, krx_ref, o_ref, s_ref):
    krc = krc_ref[...]
    krx = krx_ref[...]
    heads = []
    for h in range(MLA_STEP_HEADS):
        kn = slice(2 * h * LANE, (2 * h + 1) * LANE)
        v = slice((2 * h + 1) * LANE, (2 * h + 2) * LANE)
        heads.append((
            slice(2 * h * LANE, (2 * h + 2) * LANE),
            slice(h * LANE, (h + 1) * LANE),
            jnp.concatenate([kvc_ref[:, kn], krc], axis=1),
            jnp.concatenate([kvx_ref[:, kn], krx], axis=1),
            kvc_ref[:, v],
            kvx_ref[:, v],
        ))
    _attend(q_ref, o_ref, s_ref, heads)


def mla_attention(q, kv, kr, n_batch, seq, ctx_len):
    cb0 = n_batch * seq // ctx_len
    hw = MLA_STEP_HEADS * 2 * LANE
    return pl.pallas_call(
        _mla_body,
        grid=(n_batch, MLA_HEADS // MLA_STEP_HEADS),
        in_specs=[
            pl.BlockSpec((seq, hw), lambda b, g: (b, g)),
            pl.BlockSpec((ctx_len, hw), lambda b, g: (cb0 + b, g)),
            pl.BlockSpec((seq, hw), lambda b, g: (b, g)),
            pl.BlockSpec((ctx_len, LANE), lambda b, g: (cb0 + b, 0)),
            pl.BlockSpec((seq, LANE), lambda b, g: (b, 0)),
        ],
        out_specs=pl.BlockSpec((seq, MLA_STEP_HEADS * LANE), lambda b, g: (b, g)),
        out_shape=jax.ShapeDtypeStruct((n_batch * seq, MLA_HEADS * MLA_V), BF16),
        scratch_shapes=_score_scratch(ctx_len + seq),
        compiler_params=_params(2),
        name="mla_attention",
    )(q, kv, kv, kr, kr)


def _gqa_body(q_ref, kc_ref, kx_ref, vc_ref, vx_ref, o_ref, s_ref):
    kc, kx, vc, vx = kc_ref[...], kx_ref[...], vc_ref[...], vx_ref[...]
    cols = [slice(h * LANE, (h + 1) * LANE) for h in range(q_ref.shape[1] // LANE)]
    _attend(q_ref, o_ref, s_ref, [(c, c, kc, kx, vc, vx) for c in cols])


GQA_STEP_HEADS = 1


def gqa_attention(q, k, v, n_batch, seq, ctx_len):
    cb0 = n_batch * seq // ctx_len
    qw = GQA_STEP_HEADS * LANE
    per_kv = GQA_HEADS // GQA_KV_HEADS // GQA_STEP_HEADS
    return pl.pallas_call(
        _gqa_body,
        grid=(n_batch, GQA_HEADS // GQA_STEP_HEADS),
        in_specs=[
            pl.BlockSpec((seq, qw), lambda b, g: (b, g)),
            pl.BlockSpec((ctx_len, LANE), lambda b, g: (cb0 + b, g // per_kv)),
            pl.BlockSpec((seq, LANE), lambda b, g: (b, g // per_kv)),
            pl.BlockSpec((ctx_len, LANE), lambda b, g: (cb0 + b, g // per_kv)),
            pl.BlockSpec((seq, LANE), lambda b, g: (b, g // per_kv)),
        ],
        out_specs=pl.BlockSpec((seq, qw), lambda b, g: (b, g)),
        out_shape=jax.ShapeDtypeStruct((n_batch * seq, GQA_HEADS * GQA_HEAD_DIM), BF16),
        scratch_shapes=_score_scratch(ctx_len + seq),
        compiler_params=_params(2),
        name="gqa_attention",
    )(q, k, k, v, v)


ROPE_TILE = 512


def _rope_tables(seq):
    rows = seq // GRID_W
    row = np.repeat(np.arange(rows, dtype=np.float64), GRID_W)
    col = np.tile(np.arange(GRID_W, dtype=np.float64), rows)

    def cos_sin(rot_dim):
        n_freq = rot_dim // 4
        inv = ROPE_THETA ** (-np.arange(n_freq, dtype=np.float64) / n_freq)
        ang = np.concatenate([row[:, None] * inv, col[:, None] * inv], axis=-1)
        return np.cos(ang), np.sin(ang)

    def table(x_part, ident_row):
        ident = np.broadcast_to(ident_row, (ROPE_TILE, LANE))
        return jnp.asarray(np.concatenate([x_part, ident], axis=0), F32)

    ones, zeros = np.ones((1, LANE)), np.zeros((1, LANE))
    cg, sg = cos_sin(GQA_HEAD_DIM)
    cos_g = table(np.concatenate([cg, cg], axis=1), ones)
    sin_g = table(np.concatenate([-sg, sg], axis=1), zeros)
    cm, sm = cos_sin(MLA_ROPE)
    zpad = np.zeros((seq, LANE - MLA_ROPE))
    ident_m = np.concatenate([np.ones((1, MLA_ROPE)), np.zeros((1, LANE - MLA_ROPE))], axis=1)
    cos_m = table(np.concatenate([cm, cm, zpad], axis=1), ident_m)
    sin_m = table(np.concatenate([-sm, sm, zpad], axis=1), zeros)
    sin_m_abs = table(np.concatenate([sm, sm, zpad], axis=1), zeros)
    return cos_g, sin_g, cos_m, sin_m, sin_m_abs


def _rope_block(seq, n_latent_rows):
    per_seq = seq // ROPE_TILE
    n_latent = n_latent_rows // ROPE_TILE
    return lambda i, *_: (jnp.where(i < n_latent, i % per_seq, per_seq), 0)


def _mla_q_weight(w_q_b):
    r = w_q_b.shape[0]
    w = w_q_b.reshape(r, MLA_HEADS, MLA_NOPE + MLA_ROPE)
    nope, rope = w[..., :MLA_NOPE], w[..., MLA_NOPE:]
    x1, x2 = rope[..., : MLA_ROPE // 2], rope[..., MLA_ROPE // 2 :]
    return jnp.concatenate([nope, rope, -x2, x1], axis=-1).reshape(r, MLA_HEADS * 2 * LANE).astype(BF16)


def _odd_kv_weight(w_in, q_cols):
    w = w_in[:, q_cols:]
    pad = jnp.zeros((w.shape[0], LANE - MLA_ROPE), w.dtype)
    split = MLA_KV_RANK + MLA_ROPE
    return jnp.concatenate([w[:, :split], pad, w[:, split:]], axis=1).astype(BF16)[None]


def kernel(x, c, ctx, c_ctx, mod_w, mod_b, norm1_g, norm2_g, final_g, ffn_w_gu, ffn_w_down, ev_w_in, ev_pool_w,
           ev_pool_scale, ev_sgu_norm_g, ev_sgu_w_s, ev_sgu_b, ev_w_out, od_w_in, od_q_a_g, od_w_q_b, od_kv_a_g,
           od_w_kv_b, od_q_norm_g, od_k_norm_g, od_w_out):
    B, S, D = x.shape
    CL = ctx.shape[1]
    depth = mod_w.shape[0]
    assert depth == 2, "layer schedule below is written for one even and one odd layer"
    n_x = B * S
    n_all = n_x + B * CL
    x_rows = x.reshape(n_x, D)
    c_rows = ctx.reshape(B * CL, D)

    cc = jnp.concatenate([c, c_ctx[None, :], jnp.zeros((MOD_ROWS - B - 1, D), F32)], axis=0)
    mods = Mods(adaln_all(cc, mod_w, mod_b), B, S)

    def ffn(t_in, layer, m_rows):
        h = norm_mod([t_in], norm2_g[layer], mods, layer, SH2, SC2)
        act, w_down = matmul_swiglu(h, ffn_w_gu, ffn_w_down, layer, m_rows)
        return matmul_residual([act], w_down, 0, [t_in], mods, layer, G2, m_rows, tm=512, tn=512,
                               name="ffn_down")

    pool_width = ev_pool_scale.shape[1]
    h = norm_mod([x_rows, c_rows], norm1_g[0], mods, 0, SH1, SC1)
    z = matmul(h, ev_w_in, 0, n_all, ev_w_in.shape[2], F32, name="even_in")
    ya = pool_mixer(z, ev_pool_w[0], ev_pool_scale[0], B, S, CL)
    yb = sgu_mixer(z, ev_sgu_norm_g[0], ev_sgu_w_s[0], ev_sgu_b[0], pool_width)
    t = matmul_residual([ya, yb], ev_w_out.astype(BF16), 0, [x_rows, c_rows], mods, 0, G1, n_all,
                        name="even_out")
    t = ffn(t, 0, n_all)

    q_cols = MLA_Q_RANK + GQA_HEADS * GQA_HEAD_DIM
    cos_g, sin_g, cos_m, sin_m, sin_m_abs = _rope_tables(S)
    rope_block = _rope_block(S, n_x)
    h = norm_mod([t], norm1_g[1], mods, 1, SH1, SC1)
    w_in_bf = od_w_in[0].astype(BF16)
    zq = matmul(h, w_in_bf[None, :, :q_cols], 0, n_x, q_cols, F32, name="odd_in_q")
    w_kv = _odd_kv_weight(w_in_bf, q_cols)
    zkv = matmul(h, w_kv, 0, n_all, w_kv.shape[2], F32, tm=512, tn=w_kv.shape[2], name="odd_in_kv")
    cq, gq = prep_q(zq, od_q_a_g[0], od_q_norm_g[0], cos_g, sin_g, rope_block)
    ckv, kr, gk, gv = prep_kv(zkv, od_kv_a_g[0], od_k_norm_g[0], cos_g, sin_g, cos_m, sin_m, rope_block)
    q_m = matmul_qrope(cq, _mla_q_weight(od_w_q_b[0]), cos_m, sin_m_abs, rope_block)
    kv_m = matmul(ckv, od_w_kv_b, 0, n_all, od_w_kv_b.shape[2], BF16, tm=512, tn=od_w_kv_b.shape[2],
                  name="mla_kv_up")
    o_m = mla_attention(q_m, kv_m, kr, B, S, CL)
    o_g = gqa_attention(gq, gk, gv, B, S, CL)
    t = matmul_residual([o_m, o_g], od_w_out, 0, [t], mods, 1, G1, n_x, name="odd_out")
    t = ffn(t, 1, n_x)

    return final_norm(t, final_g, n_x).reshape(B, S, D)
```

```python
import functools
import math

import jax
import jax.numpy as jnp
import numpy as np
from jax import lax
from jax.experimental import pallas as pl
from jax.experimental.pallas import tpu as pltpu

F32 = jnp.float32
BF16 = jnp.bfloat16

GRID_W = 64
EPS = 1e-6
ROPE_THETA = 10000.0
POOL_WINDOWS = (2, 4, 8, 16)
POOL_HALO = 16
SGU_CHUNK = 128
MLA_HEADS = 16
MLA_Q_RANK = 1024
MLA_KV_RANK = 512
MLA_NOPE = 128
MLA_ROPE = 64
MLA_V = 128
GQA_HEADS = 16
GQA_KV_HEADS = 4
GQA_HEAD_DIM = 128
LANE = 128
MOD_ROWS = 16
SH1, SC1, G1, SH2, SC2, G2 = range(6)

VMEM_LIMIT = 56 * 1024 * 1024


def _params(n_axes):
    return pltpu.CompilerParams(dimension_semantics=("arbitrary",) * n_axes, vmem_limit_bytes=VMEM_LIMIT)


def _silu(x):
    return x / (1.0 + jnp.exp(-x))


def _gelu_tanh(x):
    c = math.sqrt(2.0 / math.pi)
    return 0.5 * x * (1.0 + jnp.tanh(x * (c + (c * 0.044715) * (x * x))))


def _rms(x, g):
    return x * lax.rsqrt(jnp.mean(x * x, axis=-1, keepdims=True) + EPS) * g


def _row_sources(srcs, tm, width, col_of):
    n0 = srcs[0].shape[0] // tm
    if len(srcs) == 1:
        return n0, [pl.BlockSpec((tm, width), lambda i, *r: (i, col_of(*r)))]
    n1 = srcs[1].shape[0] // tm
    return n0, [
        pl.BlockSpec((tm, width), lambda i, *r: (jnp.minimum(i, n0 - 1), col_of(*r))),
        pl.BlockSpec((tm, width), lambda i, *r: (jnp.clip(i - n0, 0, n1 - 1), col_of(*r))),
    ]


def _from_source(refs, n0, i):
    if len(refs) == 1:
        return refs[0][...]
    return jnp.where(i < n0, refs[0][...], refs[1][...])


def _adaln_body(c_ref, w_ref, b_ref, o_ref):
    sc = _silu(c_ref[...]).astype(BF16)
    m = jnp.dot(sc, w_ref[0].astype(BF16), preferred_element_type=F32) + b_ref[0]
    for r in range(MOD_ROWS):
        o_ref[r] = m[r : r + 1]


def adaln_all(cc, mod_w, mod_b, tn=1024):
    L, D, N = mod_w.shape
    per_chunk = D // tn
    return pl.pallas_call(
        _adaln_body,
        grid=(L, N // tn),
        in_specs=[
            pl.BlockSpec((MOD_ROWS, D), lambda l, j: (0, 0)),
            pl.BlockSpec((1, D, tn), lambda l, j: (l, 0, j)),
            pl.BlockSpec((1, 1, tn), lambda l, j: (l, 0, j)),
        ],
        out_specs=pl.BlockSpec((MOD_ROWS, 1, tn), lambda l, j: (l * (N // D) + j // per_chunk, 0, j % per_chunk)),
        out_shape=jax.ShapeDtypeStruct((L * (N // D) * MOD_ROWS, 1, D), F32),
        compiler_params=_params(2),
        name="adaln",
    )(cc, mod_w, mod_b.reshape(L, 1, N))


class Mods:
    def __init__(self, table, n_batch, seq):
        self.table = table
        self.n_batch = n_batch
        self.seq = seq

    def index(self, layer, which, tm):
        n_x = self.n_batch * self.seq // tm
        base = (layer * 6 + which) * MOD_ROWS
        return lambda i: base + jnp.where(i < n_x, (i * tm) // self.seq, self.n_batch)


def _norm_mod_body(*refs, n_src, n0):
    g_ref, sh_ref, sc_ref, o_ref = refs[n_src:]

    def emit(x_ref):
        o_ref[...] = (_rms(x_ref[...], g_ref[...] * (1.0 + sc_ref[0])) + sh_ref[0]).astype(o_ref.dtype)

    if n_src == 1:
        emit(refs[0])
    else:
        i = pl.program_id(0)
        pl.when(i < n0)(lambda: emit(refs[0]))
        pl.when(i >= n0)(lambda: emit(refs[1]))


def norm_mod(srcs, gain, mods, layer, which_shift, which_scale, tm=512):
    D = srcs[0].shape[1]
    M = sum(s.shape[0] for s in srcs)
    sh_idx = mods.index(layer, which_shift, tm)
    sc_idx = mods.index(layer, which_scale, tm)
    n0, src_specs = _row_sources(srcs, tm, D, lambda: 0)
    return pl.pallas_call(
        functools.partial(_norm_mod_body, n_src=len(srcs), n0=n0),
        grid=(M // tm,),
        in_specs=src_specs
        + [
            pl.BlockSpec((1, D), lambda i: (0, 0)),
            pl.BlockSpec((1, 1, D), lambda i: (sh_idx(i), 0, 0)),
            pl.BlockSpec((1, 1, D), lambda i: (sc_idx(i), 0, 0)),
        ],
        out_specs=pl.BlockSpec((tm, D), lambda i: (i, 0)),
        out_shape=jax.ShapeDtypeStruct((M, D), BF16),
        compiler_params=_params(1),
        name="norm_mod",
    )(*srcs, gain.reshape(1, D), mods.table, mods.table)


def _rms_body(x_ref, g_ref, o_ref):
    o_ref[...] = _rms(x_ref[...], g_ref[...]).astype(o_ref.dtype)


def final_norm(t, gain, m_rows, tm=512):
    D = t.shape[1]
    return pl.pallas_call(
        _rms_body,
        grid=(m_rows // tm,),
        in_specs=[pl.BlockSpec((tm, D), lambda i: (i, 0)), pl.BlockSpec((1, D), lambda i: (0, 0))],
        out_specs=pl.BlockSpec((tm, D), lambda i: (i, 0)),
        out_shape=jax.ShapeDtypeStruct((m_rows, D), F32),
        compiler_params=_params(1),
        name="final_norm",
    )(t, gain.reshape(1, D))


def _mm_body(a_ref, w_ref, o_ref):
    o_ref[...] = jnp.dot(a_ref[...], w_ref[...].astype(BF16), preferred_element_type=F32).astype(o_ref.dtype)


def matmul(a, w, wl, m_rows, n_cols, out_dtype, tm=1024, tn=512, name="mm"):
    K = w.shape[1]
    return pl.pallas_call(
        _mm_body,
        grid=(m_rows // tm, n_cols // tn),
        in_specs=[
            pl.BlockSpec((tm, K), lambda i, j: (i, 0)),
            pl.BlockSpec((None, K, tn), lambda i, j: (wl, 0, j)),
        ],
        out_specs=pl.BlockSpec((tm, tn), lambda i, j: (i, j)),
        out_shape=jax.ShapeDtypeStruct((m_rows, n_cols), out_dtype),
        compiler_params=_params(2),
        name=name,
    )(a, w)


def _mm_res_body(*refs, n_a, n_res, n0):
    a_refs, w_refs = refs[:n_a], refs[n_a : 2 * n_a]
    r_refs = refs[2 * n_a : 2 * n_a + n_res]
    g_ref, o_ref = refs[2 * n_a + n_res :]
    acc = jnp.dot(a_refs[0][...], w_refs[0][...].astype(BF16), preferred_element_type=F32)
    for a_ref, w_ref in zip(a_refs[1:], w_refs[1:]):
        acc = acc + jnp.dot(a_ref[...], w_ref[...].astype(BF16), preferred_element_type=F32)
    o_ref[...] = _from_source(r_refs, n0, pl.program_id(0)) + g_ref[0] * acc


def matmul_residual(a_parts, w, wl, res_srcs, mods, layer, which_gate, m_rows, tm=1024, tn=512, name="mm_res"):
    K, N = w.shape[1:]
    kp = K // len(a_parts)
    g_idx = mods.index(layer, which_gate, tm)
    n0, res_specs = _row_sources(res_srcs, tm, tn, lambda j: j)
    a_specs = [pl.BlockSpec((tm, kp), lambda i, j: (i, 0)) for _ in a_parts]
    w_specs = [pl.BlockSpec((None, kp, tn), lambda i, j, p=p: (wl, p, j)) for p in range(len(a_parts))]
    return pl.pallas_call(
        functools.partial(_mm_res_body, n_a=len(a_parts), n_res=len(res_srcs), n0=n0),
        grid=(m_rows // tm, N // tn),
        in_specs=a_specs + w_specs + res_specs + [pl.BlockSpec((1, 1, tn), lambda i, j: (g_idx(i), 0, j))],
        out_specs=pl.BlockSpec((tm, tn), lambda i, j: (i, j)),
        out_shape=jax.ShapeDtypeStruct((m_rows, N), F32),
        compiler_params=_params(2),
        name=name,
    )(*a_parts, *([w] * len(a_parts)), *res_srcs, mods.table)


def _swiglu_body(a_ref, wg_ref, wu_ref, wd_ref, o_ref, wd_bf_ref):
    a = a_ref[...]
    g = jnp.dot(a, wg_ref[...].astype(BF16), preferred_element_type=F32)
    u = jnp.dot(a, wu_ref[...].astype(BF16), preferred_element_type=F32)
    o_ref[...] = (_silu(g) * u).astype(o_ref.dtype)
    @pl.when(pl.program_id(0) == 0)
    def _():
        wd_bf_ref[...] = wd_ref[...].astype(BF16)


def matmul_swiglu(a, w_gu, w_down, wl, m_rows, tm=1024, tn=256):
    K, N2 = w_gu.shape[1:]
    H, Nd = w_down.shape[1:]
    assert N2 == 2 * H
    nj = H // tn
    wd_block = lambda i, j: jnp.where(i == 0, j, nj - 1)
    return pl.pallas_call(
        _swiglu_body,
        grid=(m_rows // tm, nj),
        in_specs=[
            pl.BlockSpec((tm, K), lambda i, j: (i, 0)),
            pl.BlockSpec((None, K, tn), lambda i, j: (wl, 0, j)),
            pl.BlockSpec((None, K, tn), lambda i, j: (wl, 0, j + nj)),
            pl.BlockSpec((None, tn, Nd), lambda i, j: (wl, wd_block(i, j), 0)),
        ],
        out_specs=[
            pl.BlockSpec((tm, tn), lambda i, j: (i, j)),
            pl.BlockSpec((None, tn, Nd), lambda i, j: (0, wd_block(i, j), 0)),
        ],
        out_shape=[jax.ShapeDtypeStruct((m_rows, H), BF16), jax.ShapeDtypeStruct((1, H, Nd), BF16)],
        compiler_params=_params(2),
        name="ffn_up",
    )(a, w_gu, w_gu, w_down)


def _pool_body(z_ref, w_ref, s_ref, o_ref, *, n_batch, ctx_len):
    b = pl.program_id(0)
    g = pl.program_id(1)
    rows, C = z_ref.shape

    def run(sub_len):
        n_sub = rows // sub_len
        stride = sub_len + POOL_HALO
        z = z_ref[...]
        gap = jnp.zeros((POOL_HALO, C), F32)
        zp = jnp.concatenate([p for k in range(n_sub) for p in (z[k * sub_len : (k + 1) * sub_len], gap)], axis=0)
        total = n_sub * stride
        t1 = lax.broadcasted_iota(jnp.int32, (sub_len, 1), 0)
        t = t1 if n_sub == 1 else jnp.concatenate([t1] * n_sub, axis=0)

        def prev(a, k):
            return pltpu.roll(a, k, 0)

        def nxt(a, k):
            return pltpu.roll(a, total - k, 0)

        def window_sum(level):
            s = zp + prev(zp, 1)
            for lv in range(level):
                s = prev(s, 2**lv) + nxt(s, 2**lv)
            if n_sub == 1:
                return s[:sub_len]
            return jnp.concatenate([s[k * stride : k * stride + sub_len] for k in range(n_sub)], axis=0)

        for gi, window in enumerate(POOL_WINDOWS):

            @pl.when(g == gi)
            def _(gi=gi, window=window):
                half = window // 2
                cnt = (jnp.minimum(t - half + window, sub_len) - jnp.maximum(t - half, 0)).astype(F32)
                pooled = window_sum(gi) * (1.0 / cnt) - z
                mixed = jnp.dot(pooled.astype(BF16), w_ref[0].astype(BF16), preferred_element_type=F32)
                o_ref[...] = (mixed * s_ref[...]).astype(o_ref.dtype)

    pl.when(b < n_batch)(lambda: run(rows))
    pl.when(b >= n_batch)(lambda: run(ctx_len))


def pool_mixer(z, pool_w, pool_scale, n_batch, seq, ctx_len):
    M = z.shape[0]
    n_groups, G, _ = pool_w.shape
    assert (M - n_batch * seq) % seq == 0 and seq % ctx_len == 0
    return pl.pallas_call(
        functools.partial(_pool_body, n_batch=n_batch, ctx_len=ctx_len),
        grid=(M // seq, n_groups),
        in_specs=[
            pl.BlockSpec((seq, G), lambda b, g: (b, g)),
            pl.BlockSpec((1, G, G), lambda b, g: (g, 0, 0)),
            pl.BlockSpec((1, G), lambda b, g: (0, g)),
        ],
        out_specs=pl.BlockSpec((seq, G), lambda b, g: (b, g)),
        out_shape=jax.ShapeDtypeStruct((M, n_groups * G), BF16),
        compiler_params=_params(2),
        name="pool_mixer",
    )(z, pool_w, pool_scale.reshape(1, n_groups * G))


def _sgu_body(u_ref, v_ref, ng_ref, ws_ref, bs_ref, o_ref):
    ws = ws_ref[0].astype(BF16)
    bs = bs_ref[0]
    ng = ng_ref[0]
    for c in range(u_ref.shape[0] // SGU_CHUNK):
        rows = pl.ds(c * SGU_CHUNK, SGU_CHUNK)
        vn = _rms(_gelu_tanh(v_ref[rows, :]), ng)
        gate = jnp.dot(ws, vn.astype(BF16), preferred_element_type=F32) + bs
        o_ref[rows, :] = (_gelu_tanh(u_ref[rows, :]) * gate).astype(o_ref.dtype)


def sgu_mixer(z, norm_g, w_s, b_s, col0, ts=2048):
    M = z.shape[0]
    H, P, _ = w_s.shape
    hd = norm_g.shape[1]
    cb0 = col0 // hd
    return pl.pallas_call(
        _sgu_body,
        grid=(M // ts, H),
        in_specs=[
            pl.BlockSpec((ts, hd), lambda i, h: (i, cb0 + h)),
            pl.BlockSpec((ts, hd), lambda i, h: (i, cb0 + H + h)),
            pl.BlockSpec((1, 1, hd), lambda i, h: (h, 0, 0)),
            pl.BlockSpec((1, P, P), lambda i, h: (h, 0, 0)),
            pl.BlockSpec((1, P, 1), lambda i, h: (h, 0, 0)),
        ],
        out_specs=pl.BlockSpec((ts, hd), lambda i, h: (i, h)),
        out_shape=jax.ShapeDtypeStruct((M, H * hd), BF16),
        compiler_params=_params(2),
        name="sgu_mixer",
    )(z, z, norm_g.reshape(H, 1, hd), w_s, b_s.reshape(H, P, 1))


def _rope_half(x, cos, sin_signed):
    return x * cos + pltpu.roll(x, LANE // 2, 1) * sin_signed


def _prep_q_body(zq_ref, qag_ref, qng_ref, cos_ref, sin_ref, cq_ref, gq_ref):
    cq_ref[...] = _rms(zq_ref[:, :MLA_Q_RANK], qag_ref[...]).astype(cq_ref.dtype)
    qs = _exp2_scale(1.0 / math.sqrt(GQA_HEAD_DIM))
    cos = cos_ref[...] * qs
    sin = sin_ref[...] * qs
    g = qng_ref[...]
    for h in range(GQA_HEADS):
        x = _rms(zq_ref[:, MLA_Q_RANK + h * LANE : MLA_Q_RANK + (h + 1) * LANE], g)
        gq_ref[:, h * LANE : (h + 1) * LANE] = _rope_half(x, cos, sin).astype(gq_ref.dtype)


def prep_q(zq, q_a_g, q_norm_g, cos_g, sin_g, rope_block):
    M, N = zq.shape
    tm = ROPE_TILE
    nq = GQA_HEADS * GQA_HEAD_DIM
    return pl.pallas_call(
        _prep_q_body,
        grid=(M // tm,),
        in_specs=[
            pl.BlockSpec((tm, N), lambda i: (i, 0)),
            pl.BlockSpec((1, MLA_Q_RANK), lambda i: (0, 0)),
            pl.BlockSpec((1, LANE), lambda i: (0, 0)),
            pl.BlockSpec((tm, LANE), rope_block),
            pl.BlockSpec((tm, LANE), rope_block),
        ],
        out_specs=[pl.BlockSpec((tm, MLA_Q_RANK), lambda i: (i, 0)), pl.BlockSpec((tm, nq), lambda i: (i, 0))],
        out_shape=[jax.ShapeDtypeStruct((M, MLA_Q_RANK), BF16), jax.ShapeDtypeStruct((M, nq), BF16)],
        compiler_params=_params(1),
        name="prep_q",
    )(zq, q_a_g.reshape(1, -1), q_norm_g.reshape(1, -1), cos_g, sin_g)


def _prep_kv_body(z_ref, kag_ref, kng_ref, cg_ref, sg_ref, cm_ref, sm_ref, ckv_ref, kr_ref, gk_ref, gv_ref):
    ckv_ref[...] = _rms(z_ref[:, :MLA_KV_RANK], kag_ref[...]).astype(ckv_ref.dtype)
    kr = z_ref[:, MLA_KV_RANK : MLA_KV_RANK + LANE]
    lane = lax.broadcasted_iota(jnp.int32, kr.shape, 1)
    half = MLA_ROPE // 2
    swapped = jnp.where(lane < half, pltpu.roll(kr, LANE - half, 1), pltpu.roll(kr, half, 1))
    kr_ref[...] = (kr * cm_ref[...] + swapped * sm_ref[...]).astype(kr_ref.dtype)
    off_k = MLA_KV_RANK + LANE
    off_v = off_k + GQA_KV_HEADS * LANE
    cos = cg_ref[...]
    sin = sg_ref[...]
    g = kng_ref[...]
    for h in range(GQA_KV_HEADS):
        x = _rms(z_ref[:, off_k + h * LANE : off_k + (h + 1) * LANE], g)
        gk_ref[:, h * LANE : (h + 1) * LANE] = _rope_half(x, cos, sin).astype(gk_ref.dtype)
    gv_ref[...] = z_ref[:, off_v : off_v + GQA_KV_HEADS * LANE].astype(gv_ref.dtype)


def prep_kv(zkv, kv_a_g, k_norm_g, cos_g, sin_g, cos_m, sin_m, rope_block):
    M, N = zkv.shape
    tm = ROPE_TILE
    nk = GQA_KV_HEADS * GQA_HEAD_DIM
    tok = lambda i: (i, 0)
    fixed = lambda i: (0, 0)
    return pl.pallas_call(
        _prep_kv_body,
        grid=(M // tm,),
        in_specs=[
            pl.BlockSpec((tm, N), tok),
            pl.BlockSpec((1, MLA_KV_RANK), fixed),
            pl.BlockSpec((1, LANE), fixed),
            pl.BlockSpec((tm, LANE), rope_block),
            pl.BlockSpec((tm, LANE), rope_block),
            pl.BlockSpec((tm, LANE), rope_block),
            pl.BlockSpec((tm, LANE), rope_block),
        ],
        out_specs=[
            pl.BlockSpec((tm, MLA_KV_RANK), tok),
            pl.BlockSpec((tm, LANE), tok),
            pl.BlockSpec((tm, nk), tok),
            pl.BlockSpec((tm, nk), tok),
        ],
        out_shape=[
            jax.ShapeDtypeStruct((M, MLA_KV_RANK), BF16),
            jax.ShapeDtypeStruct((M, LANE), BF16),
            jax.ShapeDtypeStruct((M, nk), BF16),
            jax.ShapeDtypeStruct((M, nk), BF16),
        ],
        compiler_params=_params(1),
        name="prep_kv",
    )(zkv, kv_a_g.reshape(1, -1), k_norm_g.reshape(1, -1), cos_g, sin_g, cos_m, sin_m)


def _mm_qrope_body(a_ref, w_ref, c_ref, s_ref, o_ref):
    acc = jnp.dot(a_ref[...], w_ref[...], preferred_element_type=F32) * _exp2_scale(
        1.0 / math.sqrt(MLA_NOPE + MLA_ROPE)
    )
    cos = c_ref[...]
    sin = s_ref[...]
    for h in range(acc.shape[1] // (2 * LANE)):
        lo = h * 2 * LANE
        o_ref[:, lo : lo + LANE] = acc[:, lo : lo + LANE].astype(o_ref.dtype)
        t = acc[:, lo + LANE : lo + 2 * LANE]
        o_ref[:, lo + LANE : lo + 2 * LANE] = (t * cos + pltpu.roll(t, LANE // 2, 1) * sin).astype(o_ref.dtype)


def matmul_qrope(a, w, cos_m, sin_m_abs, rope_block):
    M, K = a.shape
    tm = ROPE_TILE
    N = tn = w.shape[1]
    return pl.pallas_call(
        _mm_qrope_body,
        grid=(M // tm, N // tn),
        in_specs=[
            pl.BlockSpec((tm, K), lambda i, j: (i, 0)),
            pl.BlockSpec((K, tn), lambda i, j: (0, j)),
            pl.BlockSpec((tm, LANE), rope_block),
            pl.BlockSpec((tm, LANE), rope_block),
        ],
        out_specs=pl.BlockSpec((tm, tn), lambda i, j: (i, j)),
        out_shape=jax.ShapeDtypeStruct((M, N), BF16),
        compiler_params=_params(2),
        name="mla_q_up",
    )(a, w, cos_m, sin_m_abs)


_NT = (((1,), (1,)), ((), ()))
ATTN_SUB = 1024


def _exp2_scale(scale):
    return scale * math.log2(math.e)


def _attend(q_ref, kc, kx, vc, vx, o_ref, s_ref):
    n_sub = q_ref.shape[0] // ATTN_SUB
    n_c = kc.shape[0]
    vc, vx = _with_ones(vc), _with_ones(vx)

    def scores(r):
        slot = r % s_ref.shape[0]
        q = q_ref[pl.ds(r * ATTN_SUB, ATTN_SUB), :]
        s_ref[slot, :, :n_c] = lax.dot_general(q, kc, _NT, preferred_element_type=F32)
        s_ref[slot, :, n_c:] = lax.dot_general(q, kx, _NT, preferred_element_type=F32)
        s = s_ref[slot]
        return s, jnp.max(s, axis=-1, keepdims=True)

    nxt = scores(0)
    for r in range(n_sub):
        s, m = nxt
        p = jnp.exp2(s - m).astype(BF16)
        if r + 1 < n_sub:
            nxt = scores(r + 1)
        o = jnp.dot(p[:, :n_c], vc, preferred_element_type=F32)
        o = o + jnp.dot(p[:, n_c:], vx, preferred_element_type=F32)
        d_v = o.shape[1] - LANE
        inv = 1.0 / o[:, d_v : d_v + 1]
        o_ref[pl.ds(r * ATTN_SUB, ATTN_SUB), :] = (o[:, :d_v] * inv).astype(o_ref.dtype)


def _with_ones(v):
    return jnp.concatenate([v, jnp.ones((v.shape[0], LANE), v.dtype)], axis=1)


def _score_scratch(n_keys):
    return [pltpu.VMEM((2, ATTN_SUB, n_keys), F32)]


def _mla_body(q_ref, kvc_ref, kvx_ref, krc_ref, krx_ref, o_ref, s_ref):
    kc = jnp.concatenate([kvc_ref[:, :LANE], krc_ref[...]], axis=1)
    kx = jnp.concatenate([kvx_ref[:, :LANE], krx_ref[...]], axis=1)
    _attend(q_ref, kc, kx, kvc_ref[:, LANE:], kvx_ref[:, LANE:], o_ref, s_ref)


def mla_attention(q, kv, kr, n_batch, seq, ctx_len):
    cb0 = n_batch * seq // ctx_len
    return pl.pallas_call(
        _mla_body,
        grid=(n_batch, MLA_HEADS),
        in_specs=[
            pl.BlockSpec((seq, 2 * LANE), lambda b, h: (b, h)),
            pl.BlockSpec((ctx_len, 2 * LANE), lambda b, h: (cb0 + b, h)),
            pl.BlockSpec((seq, 2 * LANE), lambda b, h: (b, h)),
            pl.BlockSpec((ctx_len, LANE), lambda b, h: (cb0 + b, 0)),
            pl.BlockSpec((seq, LANE), lambda b, h: (b, 0)),
        ],
        out_specs=pl.BlockSpec((seq, LANE), lambda b, h: (b, h)),
        out_shape=jax.ShapeDtypeStruct((n_batch * seq, MLA_HEADS * MLA_V), BF16),
        scratch_shapes=_score_scratch(ctx_len + seq),
        compiler_params=_params(2),
        name="mla_attention",
    )(q, kv, kv, kr, kr)


def _gqa_body(q_ref, kc_ref, kx_ref, vc_ref, vx_ref, o_ref, s_ref):
    _attend(q_ref, kc_ref[...], kx_ref[...], vc_ref[...], vx_ref[...], o_ref, s_ref)


def gqa_attention(q, k, v, n_batch, seq, ctx_len):
    cb0 = n_batch * seq // ctx_len
    rep = GQA_HEADS // GQA_KV_HEADS
    return pl.pallas_call(
        _gqa_body,
        grid=(n_batch, GQA_HEADS),
        in_specs=[
            pl.BlockSpec((seq, LANE), lambda b, h: (b, h)),
            pl.BlockSpec((ctx_len, LANE), lambda b, h: (cb0 + b, h // rep)),
            pl.BlockSpec((seq, LANE), lambda b, h: (b, h // rep)),
            pl.BlockSpec((ctx_len, LANE), lambda b, h: (cb0 + b, h // rep)),
            pl.BlockSpec((seq, LANE), lambda b, h: (b, h // rep)),
        ],
        out_specs=pl.BlockSpec((seq, LANE), lambda b, h: (b, h)),
        out_shape=jax.ShapeDtypeStruct((n_batch * seq, GQA_HEADS * GQA_HEAD_DIM), BF16),
        scratch_shapes=_score_scratch(ctx_len + seq),
        compiler_params=_params(2),
        name="gqa_attention",
    )(q, k, k, v, v)


ROPE_TILE = 512


def _rope_tables(seq):
    rows = seq // GRID_W
    row = np.repeat(np.arange(rows, dtype=np.float64), GRID_W)
    col = np.tile(np.arange(GRID_W, dtype=np.float64), rows)

    def cos_sin(rot_dim):
        n_freq = rot_dim // 4
        inv = ROPE_THETA ** (-np.arange(n_freq, dtype=np.float64) / n_freq)
        ang = np.concatenate([row[:, None] * inv, col[:, None] * inv], axis=-1)
        return np.cos(ang), np.sin(ang)

    def table(x_part, ident_row):
        ident = np.broadcast_to(ident_row, (ROPE_TILE, LANE))
        return jnp.asarray(np.concatenate([x_part, ident], axis=0), F32)

    ones, zeros = np.ones((1, LANE)), np.zeros((1, LANE))
    cg, sg = cos_sin(GQA_HEAD_DIM)
    cos_g = table(np.concatenate([cg, cg], axis=1), ones)
    sin_g = table(np.concatenate([-sg, sg], axis=1), zeros)
    cm, sm = cos_sin(MLA_ROPE)
    zpad = np.zeros((seq, LANE - MLA_ROPE))
    ident_m = np.concatenate([np.ones((1, MLA_ROPE)), np.zeros((1, LANE - MLA_ROPE))], axis=1)
    cos_m = table(np.concatenate([cm, cm, zpad], axis=1), ident_m)
    sin_m = table(np.concatenate([-sm, sm, zpad], axis=1), zeros)
    sin_m_abs = table(np.concatenate([sm, sm, zpad], axis=1), zeros)
    return cos_g, sin_g, cos_m, sin_m, sin_m_abs


def _rope_block(seq, n_latent_rows):
    per_seq = seq // ROPE_TILE
    n_latent = n_latent_rows // ROPE_TILE
    return lambda i, *_: (jnp.where(i < n_latent, i % per_seq, per_seq), 0)


def _mla_q_weight(w_q_b):
    r = w_q_b.shape[0]
    w = w_q_b.reshape(r, MLA_HEADS, MLA_NOPE + MLA_ROPE)
    nope, rope = w[..., :MLA_NOPE], w[..., MLA_NOPE:]
    x1, x2 = rope[..., : MLA_ROPE // 2], rope[..., MLA_ROPE // 2 :]
    return jnp.concatenate([nope, rope, -x2, x1], axis=-1).reshape(r, MLA_HEADS * 2 * LANE).astype(BF16)


def _odd_kv_weight(w_in, q_cols):
    w = w_in[:, q_cols:]
    pad = jnp.zeros((w.shape[0], LANE - MLA_ROPE), w.dtype)
    split = MLA_KV_RANK + MLA_ROPE
    return jnp.concatenate([w[:, :split], pad, w[:, split:]], axis=1).astype(BF16)[None]


def kernel(x, c, ctx, c_ctx, mod_w, mod_b, norm1_g, norm2_g, final_g, ffn_w_gu, ffn_w_down, ev_w_in, ev_pool_w,
           ev_pool_scale, ev_sgu_norm_g, ev_sgu_w_s, ev_sgu_b, ev_w_out, od_w_in, od_q_a_g, od_w_q_b, od_kv_a_g,
           od_w_kv_b, od_q_norm_g, od_k_norm_g, od_w_out):
    B, S, D = x.shape
    CL = ctx.shape[1]
    depth = mod_w.shape[0]
    assert depth == 2, "layer schedule below is written for one even and one odd layer"
    n_x = B * S
    n_all = n_x + B * CL
    x_rows = x.reshape(n_x, D)
    c_rows = ctx.reshape(B * CL, D)

    cc = jnp.concatenate([c, c_ctx[None, :], jnp.zeros((MOD_ROWS - B - 1, D), F32)], axis=0)
    mods = Mods(adaln_all(cc, mod_w, mod_b), B, S)

    def ffn(t_in, layer, m_rows):
        h = norm_mod([t_in], norm2_g[layer], mods, layer, SH2, SC2)
        act, w_down = matmul_swiglu(h, ffn_w_gu, ffn_w_down, layer, m_rows)
        return matmul_residual([act], w_down, 0, [t_in], mods, layer, G2, m_rows, tm=512, tn=512,
                               name="ffn_down")

    pool_width = ev_pool_scale.shape[1]
    h = norm_mod([x_rows, c_rows], norm1_g[0], mods, 0, SH1, SC1)
    z = matmul(h, ev_w_in, 0, n_all, ev_w_in.shape[2], F32, name="even_in")
    ya = pool_mixer(z, ev_pool_w[0], ev_pool_scale[0], B, S, CL)
    yb = sgu_mixer(z, ev_sgu_norm_g[0], ev_sgu_w_s[0], ev_sgu_b[0], pool_width)
    t = matmul_residual([ya, yb], ev_w_out.astype(BF16), 0, [x_rows, c_rows], mods, 0, G1, n_all,
                        name="even_out")
    t = ffn(t, 0, n_all)

    q_cols = MLA_Q_RANK + GQA_HEADS * GQA_HEAD_DIM
    cos_g, sin_g, cos_m, sin_m, sin_m_abs = _rope_tables(S)
    rope_block = _rope_block(S, n_x)
    h = norm_mod([t], norm1_g[1], mods, 1, SH1, SC1)
    w_in_bf = od_w_in[0].astype(BF16)
    zq = matmul(h, w_in_bf[None, :, :q_cols], 0, n_x, q_cols, F32, name="odd_in_q")
    w_kv = _odd_kv_weight(w_in_bf, q_cols)
    zkv = matmul(h, w_kv, 0, n_all, w_kv.shape[2], F32, tm=512, tn=w_kv.shape[2], name="odd_in_kv")
    cq, gq = prep_q(zq, od_q_a_g[0], od_q_norm_g[0], cos_g, sin_g, rope_block)
    ckv, kr, gk, gv = prep_kv(zkv, od_kv_a_g[0], od_k_norm_g[0], cos_g, sin_g, cos_m, sin_m, rope_block)
    q_m = matmul_qrope(cq, _mla_q_weight(od_w_q_b[0]), cos_m, sin_m_abs, rope_block)
    kv_m = matmul(ckv, od_w_kv_b, 0, n_all, od_w_kv_b.shape[2], BF16, tm=512, tn=od_w_kv_b.shape[2],
                  name="mla_kv_up")
    o_m = mla_attention(q_m, kv_m, kr, B, S, CL)
    o_g = gqa_attention(gq, gk, gv, B, S, CL)
    t = matmul_residual([o_m, o_g], od_w_out.astype(BF16), 0, [t], mods, 1, G1, n_x, name="odd_out")
    t = ffn(t, 1, n_x)

    return final_norm(t, final_g, n_x).reshape(B, S, D)
```

```python
import functools
import math

import jax
import jax.numpy as jnp
import numpy as np
from jax import lax
from jax.experimental import pallas as pl
from jax.experimental.pallas import tpu as pltpu

F32 = jnp.float32
BF16 = jnp.bfloat16

GRID_W = 64
EPS = 1e-6
ROPE_THETA = 10000.0
POOL_WINDOWS = (2, 4, 8, 16)
POOL_HALO = 16
SGU_CHUNK = 128
MLA_HEADS = 16
MLA_Q_RANK = 1024
MLA_KV_RANK = 512
MLA_NOPE = 128
MLA_ROPE = 64
MLA_V = 128
GQA_HEADS = 16
GQA_KV_HEADS = 4
GQA_HEAD_DIM = 128
LANE = 128
MOD_ROWS = 16
SH1, SC1, G1, SH2, SC2, G2 = range(6)

VMEM_LIMIT = 56 * 1024 * 1024


def _params(n_axes):
    return pltpu.CompilerParams(dimension_semantics=("arbitrary",) * n_axes, vmem_limit_bytes=VMEM_LIMIT)


def _silu(x):
    return x / (1.0 + jnp.exp(-x))


def _gelu_tanh(x):
    c = math.sqrt(2.0 / math.pi)
    return 0.5 * x * (1.0 + jnp.tanh(x * (c + (c * 0.044715) * (x * x))))


def _rms(x, g):
    return x * lax.rsqrt(jnp.mean(x * x, axis=-1, keepdims=True) + EPS) * g


def _row_sources(srcs, tm, width, col_of):
    n0 = srcs[0].shape[0] // tm
    if len(srcs) == 1:
        return n0, [pl.BlockSpec((tm, width), lambda i, *r: (i, col_of(*r)))]
    n1 = srcs[1].shape[0] // tm
    return n0, [
        pl.BlockSpec((tm, width), lambda i, *r: (jnp.minimum(i, n0 - 1), col_of(*r))),
        pl.BlockSpec((tm, width), lambda i, *r: (jnp.clip(i - n0, 0, n1 - 1), col_of(*r))),
    ]


def _from_source(refs, n0, i):
    if len(refs) == 1:
        return refs[0][...]
    return jnp.where(i < n0, refs[0][...], refs[1][...])


def _adaln_body(c_ref, w_ref, b_ref, o_ref):
    sc = _silu(c_ref[...]).astype(BF16)
    m = jnp.dot(sc, w_ref[0].astype(BF16), preferred_element_type=F32) + b_ref[0]
    for r in range(MOD_ROWS):
        o_ref[r] = m[r : r + 1]


def adaln_all(cc, mod_w, mod_b, tn=1024):
    L, D, N = mod_w.shape
    per_chunk = D // tn
    return pl.pallas_call(
        _adaln_body,
        grid=(L, N // tn),
        in_specs=[
            pl.BlockSpec((MOD_ROWS, D), lambda l, j: (0, 0)),
            pl.BlockSpec((1, D, tn), lambda l, j: (l, 0, j)),
            pl.BlockSpec((1, 1, tn), lambda l, j: (l, 0, j)),
        ],
        out_specs=pl.BlockSpec((MOD_ROWS, 1, tn), lambda l, j: (l * (N // D) + j // per_chunk, 0, j % per_chunk)),
        out_shape=jax.ShapeDtypeStruct((L * (N // D) * MOD_ROWS, 1, D), F32),
        compiler_params=_params(2),
        name="adaln",
    )(cc, mod_w, mod_b.reshape(L, 1, N))


class Mods:
    def __init__(self, table, n_batch, seq):
        self.table = table
        self.n_batch = n_batch
        self.seq = seq

    def index(self, layer, which, tm):
        n_x = self.n_batch * self.seq // tm
        base = (layer * 6 + which) * MOD_ROWS
        return lambda i: base + jnp.where(i < n_x, (i * tm) // self.seq, self.n_batch)


def _norm_mod_body(*refs, n_src, n0):
    g_ref, sh_ref, sc_ref, o_ref = refs[n_src:]

    def emit(x_ref):
        o_ref[...] = (_rms(x_ref[...], g_ref[...] * (1.0 + sc_ref[0])) + sh_ref[0]).astype(o_ref.dtype)

    if n_src == 1:
        emit(refs[0])
    else:
        i = pl.program_id(0)
        pl.when(i < n0)(lambda: emit(refs[0]))
        pl.when(i >= n0)(lambda: emit(refs[1]))


def norm_mod(srcs, gain, mods, layer, which_shift, which_scale, tm=512):
    D = srcs[0].shape[1]
    M = sum(s.shape[0] for s in srcs)
    sh_idx = mods.index(layer, which_shift, tm)
    sc_idx = mods.index(layer, which_scale, tm)
    n0, src_specs = _row_sources(srcs, tm, D, lambda: 0)
    return pl.pallas_call(
        functools.partial(_norm_mod_body, n_src=len(srcs), n0=n0),
        grid=(M // tm,),
        in_specs=src_specs
        + [
            pl.BlockSpec((1, D), lambda i: (0, 0)),
            pl.BlockSpec((1, 1, D), lambda i: (sh_idx(i), 0, 0)),
            pl.BlockSpec((1, 1, D), lambda i: (sc_idx(i), 0, 0)),
        ],
        out_specs=pl.BlockSpec((tm, D), lambda i: (i, 0)),
        out_shape=jax.ShapeDtypeStruct((M, D), BF16),
        compiler_params=_params(1),
        name="norm_mod",
    )(*srcs, gain.reshape(1, D), mods.table, mods.table)


def _rms_body(x_ref, g_ref, o_ref):
    o_ref[...] = _rms(x_ref[...], g_ref[...]).astype(o_ref.dtype)


def final_norm(t, gain, m_rows, tm=512):
    D = t.shape[1]
    return pl.pallas_call(
        _rms_body,
        grid=(m_rows // tm,),
        in_specs=[pl.BlockSpec((tm, D), lambda i: (i, 0)), pl.BlockSpec((1, D), lambda i: (0, 0))],
        out_specs=pl.BlockSpec((tm, D), lambda i: (i, 0)),
        out_shape=jax.ShapeDtypeStruct((m_rows, D), F32),
        compiler_params=_params(1),
        name="final_norm",
    )(t, gain.reshape(1, D))


def _mm_body(a_ref, w_ref, o_ref):
    o_ref[...] = jnp.dot(a_ref[...], w_ref[...].astype(BF16), preferred_element_type=F32).astype(o_ref.dtype)


def matmul(a, w, wl, m_rows, n_cols, out_dtype, tm=1024, tn=512, name="mm"):
    K = w.shape[1]
    return pl.pallas_call(
        _mm_body,
        grid=(m_rows // tm, n_cols // tn),
        in_specs=[
            pl.BlockSpec((tm, K), lambda i, j: (i, 0)),
            pl.BlockSpec((None, K, tn), lambda i, j: (wl, 0, j)),
        ],
        out_specs=pl.BlockSpec((tm, tn), lambda i, j: (i, j)),
        out_shape=jax.ShapeDtypeStruct((m_rows, n_cols), out_dtype),
        compiler_params=_params(2),
        name=name,
    )(a, w)


def _mm_res_body(*refs, n_a, n_res, n0):
    a_refs, w_refs = refs[:n_a], refs[n_a : 2 * n_a]
    r_refs = refs[2 * n_a : 2 * n_a + n_res]
    g_ref, o_ref = refs[2 * n_a + n_res :]
    acc = jnp.dot(a_refs[0][...], w_refs[0][...].astype(BF16), preferred_element_type=F32)
    for a_ref, w_ref in zip(a_refs[1:], w_refs[1:]):
        acc = acc + jnp.dot(a_ref[...], w_ref[...].astype(BF16), preferred_element_type=F32)
    o_ref[...] = _from_source(r_refs, n0, pl.program_id(0)) + g_ref[0] * acc


def matmul_residual(a_parts, w, wl, res_srcs, mods, layer, which_gate, m_rows, tm=1024, tn=512, name="mm_res"):
    K, N = w.shape[1:]
    kp = K // len(a_parts)
    g_idx = mods.index(layer, which_gate, tm)
    n0, res_specs = _row_sources(res_srcs, tm, tn, lambda j: j)
    a_specs = [pl.BlockSpec((tm, kp), lambda i, j: (i, 0)) for _ in a_parts]
    w_specs = [pl.BlockSpec((None, kp, tn), lambda i, j, p=p: (wl, p, j)) for p in range(len(a_parts))]
    return pl.pallas_call(
        functools.partial(_mm_res_body, n_a=len(a_parts), n_res=len(res_srcs), n0=n0),
        grid=(m_rows // tm, N // tn),
        in_specs=a_specs + w_specs + res_specs + [pl.BlockSpec((1, 1, tn), lambda i, j: (g_idx(i), 0, j))],
        out_specs=pl.BlockSpec((tm, tn), lambda i, j: (i, j)),
        out_shape=jax.ShapeDtypeStruct((m_rows, N), F32),
        compiler_params=_params(2),
        name=name,
    )(*a_parts, *([w] * len(a_parts)), *res_srcs, mods.table)


def _swiglu_body(a_ref, wg_ref, wu_ref, wd_ref, o_ref, wd_bf_ref):
    a = a_ref[...]
    g = jnp.dot(a, wg_ref[...].astype(BF16), preferred_element_type=F32)
    u = jnp.dot(a, wu_ref[...].astype(BF16), preferred_element_type=F32)
    o_ref[...] = (_silu(g) * u).astype(o_ref.dtype)
    @pl.when(pl.program_id(0) == 0)
    def _():
        wd_bf_ref[...] = wd_ref[...].astype(BF16)


def matmul_swiglu(a, w_gu, w_down, wl, m_rows, tm=1024, tn=256):
    K, N2 = w_gu.shape[1:]
    H, Nd = w_down.shape[1:]
    assert N2 == 2 * H
    nj = H // tn
    wd_block = lambda i, j: jnp.where(i == 0, j, nj - 1)
    return pl.pallas_call(
        _swiglu_body,
        grid=(m_rows // tm, nj),
        in_specs=[
            pl.BlockSpec((tm, K), lambda i, j: (i, 0)),
            pl.BlockSpec((None, K, tn), lambda i, j: (wl, 0, j)),
            pl.BlockSpec((None, K, tn), lambda i, j: (wl, 0, j + nj)),
            pl.BlockSpec((None, tn, Nd), lambda i, j: (wl, wd_block(i, j), 0)),
        ],
        out_specs=[
            pl.BlockSpec((tm, tn), lambda i, j: (i, j)),
            pl.BlockSpec((None, tn, Nd), lambda i, j: (0, wd_block(i, j), 0)),
        ],
        out_shape=[jax.ShapeDtypeStruct((m_rows, H), BF16), jax.ShapeDtypeStruct((1, H, Nd), BF16)],
        compiler_params=_params(2),
        name="ffn_up",
    )(a, w_gu, w_gu, w_down)


def _pool_body(z_ref, w_ref, s_ref, o_ref, *, n_batch, ctx_len):
    b = pl.program_id(0)
    g = pl.program_id(1)
    rows, C = z_ref.shape

    def run(sub_len):
        n_sub = rows // sub_len
        stride = sub_len + POOL_HALO
        z = z_ref[...]
        gap = jnp.zeros((POOL_HALO, C), F32)
        zp = jnp.concatenate([p for k in range(n_sub) for p in (z[k * sub_len : (k + 1) * sub_len], gap)], axis=0)
        total = n_sub * stride
        t1 = lax.broadcasted_iota(jnp.int32, (sub_len, 1), 0)
        t = t1 if n_sub == 1 else jnp.concatenate([t1] * n_sub, axis=0)

        def prev(a, k):
            return pltpu.roll(a, k, 0)

        def nxt(a, k):
            return pltpu.roll(a, total - k, 0)

        def window_sum(level):
            s = zp + prev(zp, 1)
            for lv in range(level):
                s = prev(s, 2**lv) + nxt(s, 2**lv)
            if n_sub == 1:
                return s[:sub_len]
            return jnp.concatenate([s[k * stride : k * stride + sub_len] for k in range(n_sub)], axis=0)

        for gi, window in enumerate(POOL_WINDOWS):

            @pl.when(g == gi)
            def _(gi=gi, window=window):
                half = window // 2
                cnt = (jnp.minimum(t - half + window, sub_len) - jnp.maximum(t - half, 0)).astype(F32)
                pooled = window_sum(gi) * (1.0 / cnt) - z
                mixed = jnp.dot(pooled.astype(BF16), w_ref[0].astype(BF16), preferred_element_type=F32)
                o_ref[...] = (mixed * s_ref[...]).astype(o_ref.dtype)

    pl.when(b < n_batch)(lambda: run(rows))
    pl.when(b >= n_batch)(lambda: run(ctx_len))


def pool_mixer(z, pool_w, pool_scale, n_batch, seq, ctx_len):
    M = z.shape[0]
    n_groups, G, _ = pool_w.shape
    assert (M - n_batch * seq) % seq == 0 and seq % ctx_len == 0
    return pl.pallas_call(
        functools.partial(_pool_body, n_batch=n_batch, ctx_len=ctx_len),
        grid=(M // seq, n_groups),
        in_specs=[
            pl.BlockSpec((seq, G), lambda b, g: (b, g)),
            pl.BlockSpec((1, G, G), lambda b, g: (g, 0, 0)),
            pl.BlockSpec((1, G), lambda b, g: (0, g)),
        ],
        out_specs=pl.BlockSpec((seq, G), lambda b, g: (b, g)),
        out_shape=jax.ShapeDtypeStruct((M, n_groups * G), BF16),
        compiler_params=_params(2),
        name="pool_mixer",
    )(z, pool_w, pool_scale.reshape(1, n_groups * G))


def _sgu_body(u_ref, v_ref, ng_ref, ws_ref, bs_ref, o_ref):
    ws = ws_ref[0].astype(BF16)
    bs = bs_ref[0]
    ng = ng_ref[0]
    for c in range(u_ref.shape[0] // SGU_CHUNK):
        rows = pl.ds(c * SGU_CHUNK, SGU_CHUNK)
        vn = _rms(_gelu_tanh(v_ref[rows, :]), ng)
        gate = jnp.dot(ws, vn.astype(BF16), preferred_element_type=F32) + bs
        o_ref[rows, :] = (_gelu_tanh(u_ref[rows, :]) * gate).astype(o_ref.dtype)


def sgu_mixer(z, norm_g, w_s, b_s, col0, ts=2048):
    M = z.shape[0]
    H, P, _ = w_s.shape
    hd = norm_g.shape[1]
    cb0 = col0 // hd
    return pl.pallas_call(
        _sgu_body,
        grid=(M // ts, H),
        in_specs=[
            pl.BlockSpec((ts, hd), lambda i, h: (i, cb0 + h)),
            pl.BlockSpec((ts, hd), lambda i, h: (i, cb0 + H + h)),
            pl.BlockSpec((1, 1, hd), lambda i, h: (h, 0, 0)),
            pl.BlockSpec((1, P, P), lambda i, h: (h, 0, 0)),
            pl.BlockSpec((1, P, 1), lambda i, h: (h, 0, 0)),
        ],
        out_specs=pl.BlockSpec((ts, hd), lambda i, h: (i, h)),
        out_shape=jax.ShapeDtypeStruct((M, H * hd), BF16),
        compiler_params=_params(2),
        name="sgu_mixer",
    )(z, z, norm_g.reshape(H, 1, hd), w_s, b_s.reshape(H, P, 1))


def _rope_half(x, cos, sin_signed):
    return x * cos + pltpu.roll(x, LANE // 2, 1) * sin_signed


def _q_in_body(a_ref, w_ref, qag_ref, qng_ref, cos_ref, sin_ref, cq_ref, gq_ref):
    zq = jnp.dot(a_ref[...], w_ref[...], preferred_element_type=F32)
    cq_ref[...] = _rms(zq[:, :MLA_Q_RANK], qag_ref[...]).astype(cq_ref.dtype)
    qs = _exp2_scale(1.0 / math.sqrt(GQA_HEAD_DIM))
    cos = cos_ref[...] * qs
    sin = sin_ref[...] * qs
    g = qng_ref[...]
    for h in range(GQA_HEADS):
        x = _rms(zq[:, MLA_Q_RANK + h * LANE : MLA_Q_RANK + (h + 1) * LANE], g)
        gq_ref[:, h * LANE : (h + 1) * LANE] = _rope_half(x, cos, sin).astype(gq_ref.dtype)


def q_in(h, w_q, m_rows, q_a_g, q_norm_g, cos_g, sin_g, rope_block):
    K, N = w_q.shape
    M = m_rows
    tm = ROPE_TILE
    nq = GQA_HEADS * GQA_HEAD_DIM
    return pl.pallas_call(
        _q_in_body,
        grid=(M // tm,),
        in_specs=[
            pl.BlockSpec((tm, K), lambda i: (i, 0)),
            pl.BlockSpec((K, N), lambda i: (0, 0), pipeline_mode=pl.Buffered(1)),
            pl.BlockSpec((1, MLA_Q_RANK), lambda i: (0, 0)),
            pl.BlockSpec((1, LANE), lambda i: (0, 0)),
            pl.BlockSpec((tm, LANE), rope_block),
            pl.BlockSpec((tm, LANE), rope_block),
        ],
        out_specs=[pl.BlockSpec((tm, MLA_Q_RANK), lambda i: (i, 0)), pl.BlockSpec((tm, nq), lambda i: (i, 0))],
        out_shape=[jax.ShapeDtypeStruct((M, MLA_Q_RANK), BF16), jax.ShapeDtypeStruct((M, nq), BF16)],
        compiler_params=_params(1),
        name="odd_in_q",
    )(h, w_q, q_a_g.reshape(1, -1), q_norm_g.reshape(1, -1), cos_g, sin_g)


def _kv_in_body(a_ref, w_ref, kag_ref, kng_ref, cg_ref, sg_ref, cm_ref, sm_ref, ckv_ref, kr_ref, gk_ref, gv_ref):
    z = jnp.dot(a_ref[...], w_ref[...], preferred_element_type=F32)
    ckv_ref[...] = _rms(z[:, :MLA_KV_RANK], kag_ref[...]).astype(ckv_ref.dtype)
    kr = z[:, MLA_KV_RANK : MLA_KV_RANK + LANE]
    lane = lax.broadcasted_iota(jnp.int32, kr.shape, 1)
    half = MLA_ROPE // 2
    swapped = jnp.where(lane < half, pltpu.roll(kr, LANE - half, 1), pltpu.roll(kr, half, 1))
    kr_ref[...] = (kr * cm_ref[...] + swapped * sm_ref[...]).astype(kr_ref.dtype)
    off_k = MLA_KV_RANK + LANE
    off_v = off_k + GQA_KV_HEADS * LANE
    cos = cg_ref[...]
    sin = sg_ref[...]
    g = kng_ref[...]
    for h in range(GQA_KV_HEADS):
        x = _rms(z[:, off_k + h * LANE : off_k + (h + 1) * LANE], g)
        gk_ref[:, h * LANE : (h + 1) * LANE] = _rope_half(x, cos, sin).astype(gk_ref.dtype)
    gv_ref[...] = z[:, off_v : off_v + GQA_KV_HEADS * LANE].astype(gv_ref.dtype)


def kv_in(h, w_kv, kv_a_g, k_norm_g, cos_g, sin_g, cos_m, sin_m, rope_block):
    M, K = h.shape
    N = w_kv.shape[1]
    tm = ROPE_TILE
    nk = GQA_KV_HEADS * GQA_HEAD_DIM
    tok = lambda i: (i, 0)
    fixed = lambda i: (0, 0)
    return pl.pallas_call(
        _kv_in_body,
        grid=(M // tm,),
        in_specs=[
            pl.BlockSpec((tm, K), tok),
            pl.BlockSpec((K, N), fixed),
            pl.BlockSpec((1, MLA_KV_RANK), fixed),
            pl.BlockSpec((1, LANE), fixed),
            pl.BlockSpec((tm, LANE), rope_block),
            pl.BlockSpec((tm, LANE), rope_block),
            pl.BlockSpec((tm, LANE), rope_block),
            pl.BlockSpec((tm, LANE), rope_block),
        ],
        out_specs=[
            pl.BlockSpec((tm, MLA_KV_RANK), tok),
            pl.BlockSpec((tm, LANE), tok),
            pl.BlockSpec((tm, nk), tok),
            pl.BlockSpec((tm, nk), tok),
        ],
        out_shape=[
            jax.ShapeDtypeStruct((M, MLA_KV_RANK), BF16),
            jax.ShapeDtypeStruct((M, LANE), BF16),
            jax.ShapeDtypeStruct((M, nk), BF16),
            jax.ShapeDtypeStruct((M, nk), BF16),
        ],
        compiler_params=_params(1),
        name="odd_in_kv",
    )(h, w_kv, kv_a_g.reshape(1, -1), k_norm_g.reshape(1, -1), cos_g, sin_g, cos_m, sin_m)


def _mm_qrope_body(a_ref, w_ref, c_ref, s_ref, o_ref):
    acc = jnp.dot(a_ref[...], w_ref[...], preferred_element_type=F32) * _exp2_scale(
        1.0 / math.sqrt(MLA_NOPE + MLA_ROPE)
    )
    cos = c_ref[...]
    sin = s_ref[...]
    for h in range(acc.shape[1] // (2 * LANE)):
        lo = h * 2 * LANE
        o_ref[:, lo : lo + LANE] = acc[:, lo : lo + LANE].astype(o_ref.dtype)
        t = acc[:, lo + LANE : lo + 2 * LANE]
        o_ref[:, lo + LANE : lo + 2 * LANE] = (t * cos + pltpu.roll(t, LANE // 2, 1) * sin).astype(o_ref.dtype)


def matmul_qrope(a, w, cos_m, sin_m_abs, rope_block):
    M, K = a.shape
    tm = ROPE_TILE
    N = tn = w.shape[1]
    return pl.pallas_call(
        _mm_qrope_body,
        grid=(M // tm, N // tn),
        in_specs=[
            pl.BlockSpec((tm, K), lambda i, j: (i, 0)),
            pl.BlockSpec((K, tn), lambda i, j: (0, j)),
            pl.BlockSpec((tm, LANE), rope_block),
            pl.BlockSpec((tm, LANE), rope_block),
        ],
        out_specs=pl.BlockSpec((tm, tn), lambda i, j: (i, j)),
        out_shape=jax.ShapeDtypeStruct((M, N), BF16),
        compiler_params=_params(2),
        name="mla_q_up",
    )(a, w, cos_m, sin_m_abs)


_NT = (((1,), (1,)), ((), ()))
ATTN_SUB = 1024


def _exp2_scale(scale):
    return scale * math.log2(math.e)


def _attend(q_ref, kc, kx, vc, vx, o_ref, s_ref):
    n_sub = q_ref.shape[0] // ATTN_SUB
    n_c = kc.shape[0]
    vc, vx = _with_ones(vc), _with_ones(vx)

    def scores(r):
        slot = r % s_ref.shape[0]
        q = q_ref[pl.ds(r * ATTN_SUB, ATTN_SUB), :]
        s_ref[slot, :, :n_c] = lax.dot_general(q, kc, _NT, preferred_element_type=F32)
        s_ref[slot, :, n_c:] = lax.dot_general(q, kx, _NT, preferred_element_type=F32)
        s = s_ref[slot]
        return s, jnp.max(s, axis=-1, keepdims=True)

    nxt = scores(0)
    for r in range(n_sub):
        s, m = nxt
        p = jnp.exp2(s - m).astype(BF16)
        if r + 1 < n_sub:
            nxt = scores(r + 1)
        o = jnp.dot(p[:, :n_c], vc, preferred_element_type=F32)
        o = o + jnp.dot(p[:, n_c:], vx, preferred_element_type=F32)
        d_v = o.shape[1] - LANE
        inv = 1.0 / o[:, d_v : d_v + 1]
        o_ref[pl.ds(r * ATTN_SUB, ATTN_SUB), :] = (o[:, :d_v] * inv).astype(o_ref.dtype)


def _with_ones(v):
    return jnp.concatenate([v, jnp.ones((v.shape[0], LANE), v.dtype)], axis=1)


def _score_scratch(n_keys):
    return [pltpu.VMEM((2, ATTN_SUB, n_keys), F32)]


def _mla_body(q_ref, kvc_ref, kvx_ref, krc_ref, krx_ref, o_ref, s_ref):
    kc = jnp.concatenate([kvc_ref[:, :LANE], krc_ref[...]], axis=1)
    kx = jnp.concatenate([kvx_ref[:, :LANE], krx_ref[...]], axis=1)
    _attend(q_ref, kc, kx, kvc_ref[:, LANE:], kvx_ref[:, LANE:], o_ref, s_ref)


def mla_attention(q, kv, kr, n_batch, seq, ctx_len):
    cb0 = n_batch * seq // ctx_len
    return pl.pallas_call(
        _mla_body,
        grid=(n_batch, MLA_HEADS),
        in_specs=[
            pl.BlockSpec((seq, 2 * LANE), lambda b, h: (b, h)),
            pl.BlockSpec((ctx_len, 2 * LANE), lambda b, h: (cb0 + b, h)),
            pl.BlockSpec((seq, 2 * LANE), lambda b, h: (b, h)),
            pl.BlockSpec((ctx_len, LANE), lambda b, h: (cb0 + b, 0)),
            pl.BlockSpec((seq, LANE), lambda b, h: (b, 0)),
        ],
        out_specs=pl.BlockSpec((seq, LANE), lambda b, h: (b, h)),
        out_shape=jax.ShapeDtypeStruct((n_batch * seq, MLA_HEADS * MLA_V), BF16),
        scratch_shapes=_score_scratch(ctx_len + seq),
        compiler_params=_params(2),
        name="mla_attention",
    )(q, kv, kv, kr, kr)


def _gqa_body(q_ref, kc_ref, kx_ref, vc_ref, vx_ref, o_ref, s_ref):
    _attend(q_ref, kc_ref[...], kx_ref[...], vc_ref[...], vx_ref[...], o_ref, s_ref)


def gqa_attention(q, k, v, n_batch, seq, ctx_len):
    cb0 = n_batch * seq // ctx_len
    rep = GQA_HEADS // GQA_KV_HEADS
    return pl.pallas_call(
        _gqa_body,
        grid=(n_batch, GQA_HEADS),
        in_specs=[
            pl.BlockSpec((seq, LANE), lambda b, h: (b, h)),
            pl.BlockSpec((ctx_len, LANE), lambda b, h: (cb0 + b, h // rep)),
            pl.BlockSpec((seq, LANE), lambda b, h: (b, h // rep)),
            pl.BlockSpec((ctx_len, LANE), lambda b, h: (cb0 + b, h // rep)),
            pl.BlockSpec((seq, LANE), lambda b, h: (b, h // rep)),
        ],
        out_specs=pl.BlockSpec((seq, LANE), lambda b, h: (b, h)),
        out_shape=jax.ShapeDtypeStruct((n_batch * seq, GQA_HEADS * GQA_HEAD_DIM), BF16),
        scratch_shapes=_score_scratch(ctx_len + seq),
        compiler_params=_params(2),
        name="gqa_attention",
    )(q, k, k, v, v)


ROPE_TILE = 512


def _rope_tables(seq):
    rows = seq // GRID_W
    row = np.repeat(np.arange(rows, dtype=np.float64), GRID_W)
    col = np.tile(np.arange(GRID_W, dtype=np.float64), rows)

    def cos_sin(rot_dim):
        n_freq = rot_dim // 4
        inv = ROPE_THETA ** (-np.arange(n_freq, dtype=np.float64) / n_freq)
        ang = np.concatenate([row[:, None] * inv, col[:, None] * inv], axis=-1)
        return np.cos(ang), np.sin(ang)

    def table(x_part, ident_row):
        ident = np.broadcast_to(ident_row, (ROPE_TILE, LANE))
        return jnp.asarray(np.concatenate([x_part, ident], axis=0), F32)

    ones, zeros = np.ones((1, LANE)), np.zeros((1, LANE))
    cg, sg = cos_sin(GQA_HEAD_DIM)
    cos_g = table(np.concatenate([cg, cg], axis=1), ones)
    sin_g = table(np.concatenate([-sg, sg], axis=1), zeros)
    cm, sm = cos_sin(MLA_ROPE)
    zpad = np.zeros((seq, LANE - MLA_ROPE))
    ident_m = np.concatenate([np.ones((1, MLA_ROPE)), np.zeros((1, LANE - MLA_ROPE))], axis=1)
    cos_m = table(np.concatenate([cm, cm, zpad], axis=1), ident_m)
    sin_m = table(np.concatenate([-sm, sm, zpad], axis=1), zeros)
    sin_m_abs = table(np.concatenate([sm, sm, zpad], axis=1), zeros)
    return cos_g, sin_g, cos_m, sin_m, sin_m_abs


def _rope_block(seq, n_latent_rows):
    per_seq = seq // ROPE_TILE
    n_latent = n_latent_rows // ROPE_TILE
    return lambda i, *_: (jnp.where(i < n_latent, i % per_seq, per_seq), 0)


def _mla_q_weight(w_q_b):
    r = w_q_b.shape[0]
    w = w_q_b.reshape(r, MLA_HEADS, MLA_NOPE + MLA_ROPE)
    nope, rope = w[..., :MLA_NOPE], w[..., MLA_NOPE:]
    x1, x2 = rope[..., : MLA_ROPE // 2], rope[..., MLA_ROPE // 2 :]
    return jnp.concatenate([nope, rope, -x2, x1], axis=-1).reshape(r, MLA_HEADS * 2 * LANE).astype(BF16)


def _odd_kv_weight(w_in, q_cols):
    w = w_in[:, q_cols:]
    pad = jnp.zeros((w.shape[0], LANE - MLA_ROPE), w.dtype)
    split = MLA_KV_RANK + MLA_ROPE
    return jnp.concatenate([w[:, :split], pad, w[:, split:]], axis=1).astype(BF16)


def kernel(x, c, ctx, c_ctx, mod_w, mod_b, norm1_g, norm2_g, final_g, ffn_w_gu, ffn_w_down, ev_w_in, ev_pool_w,
           ev_pool_scale, ev_sgu_norm_g, ev_sgu_w_s, ev_sgu_b, ev_w_out, od_w_in, od_q_a_g, od_w_q_b, od_kv_a_g,
           od_w_kv_b, od_q_norm_g, od_k_norm_g, od_w_out):
    B, S, D = x.shape
    CL = ctx.shape[1]
    depth = mod_w.shape[0]
    assert depth == 2, "layer schedule below is written for one even and one odd layer"
    n_x = B * S
    n_all = n_x + B * CL
    x_rows = x.reshape(n_x, D)
    c_rows = ctx.reshape(B * CL, D)

    cc = jnp.concatenate([c, c_ctx[None, :], jnp.zeros((MOD_ROWS - B - 1, D), F32)], axis=0)
    mods = Mods(adaln_all(cc, mod_w, mod_b), B, S)

    def ffn(t_in, layer, m_rows):
        h = norm_mod([t_in], norm2_g[layer], mods, layer, SH2, SC2)
        act, w_down = matmul_swiglu(h, ffn_w_gu, ffn_w_down, layer, m_rows)
        return matmul_residual([act], w_down, 0, [t_in], mods, layer, G2, m_rows, tm=512, tn=512,
                               name="ffn_down")

    pool_width = ev_pool_scale.shape[1]
    h = norm_mod([x_rows, c_rows], norm1_g[0], mods, 0, SH1, SC1)
    z = matmul(h, ev_w_in, 0, n_all, ev_w_in.shape[2], F32, name="even_in")
    ya = pool_mixer(z, ev_pool_w[0], ev_pool_scale[0], B, S, CL)
    yb = sgu_mixer(z, ev_sgu_norm_g[0], ev_sgu_w_s[0], ev_sgu_b[0], pool_width)
    t = matmul_residual([ya, yb], ev_w_out.astype(BF16), 0, [x_rows, c_rows], mods, 0, G1, n_all,
                        name="even_out")
    t = ffn(t, 0, n_all)

    q_cols = MLA_Q_RANK + GQA_HEADS * GQA_HEAD_DIM
    cos_g, sin_g, cos_m, sin_m, sin_m_abs = _rope_tables(S)
    rope_block = _rope_block(S, n_x)
    h = norm_mod([t], norm1_g[1], mods, 1, SH1, SC1)
    w_in_bf = od_w_in[0].astype(BF16)
    cq, gq = q_in(h, w_in_bf[:, :q_cols], n_x, od_q_a_g[0], od_q_norm_g[0], cos_g, sin_g, rope_block)
    ckv, kr, gk, gv = kv_in(h, _odd_kv_weight(w_in_bf, q_cols), od_kv_a_g[0], od_k_norm_g[0], cos_g, sin_g,
                            cos_m, sin_m, rope_block)
    q_m = matmul_qrope(cq, _mla_q_weight(od_w_q_b[0]), cos_m, sin_m_abs, rope_block)
    kv_m = matmul(ckv, od_w_kv_b, 0, n_all, od_w_kv_b.shape[2], BF16, tm=512, tn=od_w_kv_b.shape[2],
                  name="mla_kv_up")
    o_m = mla_attention(q_m, kv_m, kr, B, S, CL)
    o_g = gqa_attention(gq, gk, gv, B, S, CL)
    t = matmul_residual([o_m, o_g], od_w_out.astype(BF16), 0, [t], mods, 1, G1, n_x, name="odd_out")
    t = ffn(t, 1, n_x)

    return final_norm(t, final_g, n_x).reshape(B, S, D)
```

```python
import functools
import math

import jax
import jax.numpy as jnp
import numpy as np
from jax import lax
from jax.experimental import pallas as pl
from jax.experimental.pallas import tpu as pltpu

F32 = jnp.float32
BF16 = jnp.bfloat16

GRID_W = 64
EPS = 1e-6
ROPE_THETA = 10000.0
POOL_WINDOWS = (2, 4, 8, 16)
POOL_HALO = 16
SGU_CHUNK = 128
MLA_HEADS = 16
MLA_Q_RANK = 1024
MLA_KV_RANK = 512
MLA_NOPE = 128
MLA_ROPE = 64
MLA_V = 128
GQA_HEADS = 16
GQA_KV_HEADS = 4
GQA_HEAD_DIM = 128
LANE = 128
MOD_ROWS = 16
SH1, SC1, G1, SH2, SC2, G2 = range(6)

VMEM_LIMIT = 56 * 1024 * 1024


def _params(n_axes):
    return pltpu.CompilerParams(dimension_semantics=("arbitrary",) * n_axes, vmem_limit_bytes=VMEM_LIMIT)


def _silu(x):
    return x / (1.0 + jnp.exp(-x))


def _gelu_tanh(x):
    c = math.sqrt(2.0 / math.pi)
    return 0.5 * x * (1.0 + jnp.tanh(x * (c + (c * 0.044715) * (x * x))))


def _rms(x, g):
    return x * lax.rsqrt(jnp.mean(x * x, axis=-1, keepdims=True) + EPS) * g


def _row_sources(srcs, tm, width, col_of):
    n0 = srcs[0].shape[0] // tm
    if len(srcs) == 1:
        return n0, [pl.BlockSpec((tm, width), lambda i, *r: (i, col_of(*r)))]
    n1 = srcs[1].shape[0] // tm
    return n0, [
        pl.BlockSpec((tm, width), lambda i, *r: (jnp.minimum(i, n0 - 1), col_of(*r))),
        pl.BlockSpec((tm, width), lambda i, *r: (jnp.clip(i - n0, 0, n1 - 1), col_of(*r))),
    ]


def _from_source(refs, n0, i):
    if len(refs) == 1:
        return refs[0][...]
    return jnp.where(i < n0, refs[0][...], refs[1][...])


def _adaln_body(c_ref, w_ref, b_ref, o_ref):
    sc = _silu(c_ref[...]).astype(BF16)
    m = jnp.dot(sc, w_ref[0].astype(BF16), preferred_element_type=F32) + b_ref[0]
    for r in range(MOD_ROWS):
        o_ref[r] = m[r : r + 1]


def adaln_all(cc, mod_w, mod_b, tn=1024):
    L, D, N = mod_w.shape
    per_chunk = D // tn
    return pl.pallas_call(
        _adaln_body,
        grid=(L, N // tn),
        in_specs=[
            pl.BlockSpec((MOD_ROWS, D), lambda l, j: (0, 0)),
            pl.BlockSpec((1, D, tn), lambda l, j: (l, 0, j)),
            pl.BlockSpec((1, 1, tn), lambda l, j: (l, 0, j)),
        ],
        out_specs=pl.BlockSpec((MOD_ROWS, 1, tn), lambda l, j: (l * (N // D) + j // per_chunk, 0, j % per_chunk)),
        out_shape=jax.ShapeDtypeStruct((L * (N // D) * MOD_ROWS, 1, D), F32),
        compiler_params=_params(2),
        name="adaln",
    )(cc, mod_w, mod_b.reshape(L, 1, N))


class Mods:
    def __init__(self, table, n_batch, seq):
        self.table = table
        self.n_batch = n_batch
        self.seq = seq

    def index(self, layer, which, tm):
        n_x = self.n_batch * self.seq // tm
        base = (layer * 6 + which) * MOD_ROWS
        return lambda i: base + jnp.where(i < n_x, (i * tm) // self.seq, self.n_batch)


def _norm_mod_body(*refs, n_src, n0):
    g_ref, sh_ref, sc_ref, o_ref = refs[n_src:]

    def emit(x_ref):
        o_ref[...] = (_rms(x_ref[...], g_ref[...] * (1.0 + sc_ref[0])) + sh_ref[0]).astype(o_ref.dtype)

    if n_src == 1:
        emit(refs[0])
    else:
        i = pl.program_id(0)
        pl.when(i < n0)(lambda: emit(refs[0]))
        pl.when(i >= n0)(lambda: emit(refs[1]))


def norm_mod(srcs, gain, mods, layer, which_shift, which_scale, tm=512):
    D = srcs[0].shape[1]
    M = sum(s.shape[0] for s in srcs)
    sh_idx = mods.index(layer, which_shift, tm)
    sc_idx = mods.index(layer, which_scale, tm)
    n0, src_specs = _row_sources(srcs, tm, D, lambda: 0)
    return pl.pallas_call(
        functools.partial(_norm_mod_body, n_src=len(srcs), n0=n0),
        grid=(M // tm,),
        in_specs=src_specs
        + [
            pl.BlockSpec((1, D), lambda i: (0, 0)),
            pl.BlockSpec((1, 1, D), lambda i: (sh_idx(i), 0, 0)),
            pl.BlockSpec((1, 1, D), lambda i: (sc_idx(i), 0, 0)),
        ],
        out_specs=pl.BlockSpec((tm, D), lambda i: (i, 0)),
        out_shape=jax.ShapeDtypeStruct((M, D), BF16),
        compiler_params=_params(1),
        name="norm_mod",
    )(*srcs, gain.reshape(1, D), mods.table, mods.table)


def _rms_body(x_ref, g_ref, o_ref):
    o_ref[...] = _rms(x_ref[...], g_ref[...]).astype(o_ref.dtype)


def final_norm(t, gain, m_rows, tm=512):
    D = t.shape[1]
    return pl.pallas_call(
        _rms_body,
        grid=(m_rows // tm,),
        in_specs=[pl.BlockSpec((tm, D), lambda i: (i, 0)), pl.BlockSpec((1, D), lambda i: (0, 0))],
        out_specs=pl.BlockSpec((tm, D), lambda i: (i, 0)),
        out_shape=jax.ShapeDtypeStruct((m_rows, D), F32),
        compiler_params=_params(1),
        name="final_norm",
    )(t, gain.reshape(1, D))


def _mm_body(a_ref, w_ref, o_ref):
    o_ref[...] = jnp.dot(a_ref[...], w_ref[...].astype(BF16), preferred_element_type=F32).astype(o_ref.dtype)


def matmul(a, w, wl, m_rows, n_cols, out_dtype, tm=1024, tn=512, name="mm"):
    K = w.shape[1]
    return pl.pallas_call(
        _mm_body,
        grid=(m_rows // tm, n_cols // tn),
        in_specs=[
            pl.BlockSpec((tm, K), lambda i, j: (i, 0)),
            pl.BlockSpec((None, K, tn), lambda i, j: (wl, 0, j)),
        ],
        out_specs=pl.BlockSpec((tm, tn), lambda i, j: (i, j)),
        out_shape=jax.ShapeDtypeStruct((m_rows, n_cols), out_dtype),
        compiler_params=_params(2),
        name=name,
    )(a, w)


def _mm_res_body(*refs, n_a, n_res, n0):
    a_refs, w_refs = refs[:n_a], refs[n_a : 2 * n_a]
    r_refs = refs[2 * n_a : 2 * n_a + n_res]
    g_ref, o_ref = refs[2 * n_a + n_res :]
    acc = jnp.dot(a_refs[0][...], w_refs[0][...].astype(BF16), preferred_element_type=F32)
    for a_ref, w_ref in zip(a_refs[1:], w_refs[1:]):
        acc = acc + jnp.dot(a_ref[...], w_ref[...].astype(BF16), preferred_element_type=F32)
    o_ref[...] = _from_source(r_refs, n0, pl.program_id(0)) + g_ref[0] * acc


def matmul_residual(a_parts, w, wl, res_srcs, mods, layer, which_gate, m_rows, tm=1024, tn=512, name="mm_res"):
    K, N = w.shape[1:]
    kp = K // len(a_parts)
    g_idx = mods.index(layer, which_gate, tm)
    n0, res_specs = _row_sources(res_srcs, tm, tn, lambda j: j)
    a_specs = [pl.BlockSpec((tm, kp), lambda i, j: (i, 0)) for _ in a_parts]
    w_specs = [pl.BlockSpec((None, kp, tn), lambda i, j, p=p: (wl, p, j)) for p in range(len(a_parts))]
    return pl.pallas_call(
        functools.partial(_mm_res_body, n_a=len(a_parts), n_res=len(res_srcs), n0=n0),
        grid=(m_rows // tm, N // tn),
        in_specs=a_specs + w_specs + res_specs + [pl.BlockSpec((1, 1, tn), lambda i, j: (g_idx(i), 0, j))],
        out_specs=pl.BlockSpec((tm, tn), lambda i, j: (i, j)),
        out_shape=jax.ShapeDtypeStruct((m_rows, N), F32),
        compiler_params=_params(2),
        name=name,
    )(*a_parts, *([w] * len(a_parts)), *res_srcs, mods.table)


def _swiglu_body(a_ref, wg_ref, wu_ref, wd_ref, o_ref, wd_bf_ref):
    a = a_ref[...]
    g = jnp.dot(a, wg_ref[...].astype(BF16), preferred_element_type=F32)
    u = jnp.dot(a, wu_ref[...].astype(BF16), preferred_element_type=F32)
    o_ref[...] = (_silu(g) * u).astype(o_ref.dtype)
    @pl.when(pl.program_id(0) == 0)
    def _():
        wd_bf_ref[...] = wd_ref[...].astype(BF16)


def matmul_swiglu(a, w_gu, w_down, wl, m_rows, tm=1024, tn=256):
    K, N2 = w_gu.shape[1:]
    H, Nd = w_down.shape[1:]
    assert N2 == 2 * H
    nj = H // tn
    wd_block = lambda i, j: jnp.where(i == 0, j, nj - 1)
    return pl.pallas_call(
        _swiglu_body,
        grid=(m_rows // tm, nj),
        in_specs=[
            pl.BlockSpec((tm, K), lambda i, j: (i, 0)),
            pl.BlockSpec((None, K, tn), lambda i, j: (wl, 0, j)),
            pl.BlockSpec((None, K, tn), lambda i, j: (wl, 0, j + nj)),
            pl.BlockSpec((None, tn, Nd), lambda i, j: (wl, wd_block(i, j), 0)),
        ],
        out_specs=[
            pl.BlockSpec((tm, tn), lambda i, j: (i, j)),
            pl.BlockSpec((None, tn, Nd), lambda i, j: (0, wd_block(i, j), 0)),
        ],
        out_shape=[jax.ShapeDtypeStruct((m_rows, H), BF16), jax.ShapeDtypeStruct((1, H, Nd), BF16)],
        compiler_params=_params(2),
        name="ffn_up",
    )(a, w_gu, w_gu, w_down)


def _pool_body(z_ref, w_ref, s_ref, o_ref, *, n_batch, ctx_len):
    b = pl.program_id(0)
    g = pl.program_id(1)
    rows, C = z_ref.shape

    def run(sub_len):
        n_sub = rows // sub_len
        stride = sub_len + POOL_HALO
        z = z_ref[...]
        gap = jnp.zeros((POOL_HALO, C), F32)
        zp = jnp.concatenate([p for k in range(n_sub) for p in (z[k * sub_len : (k + 1) * sub_len], gap)], axis=0)
        total = n_sub * stride
        t1 = lax.broadcasted_iota(jnp.int32, (sub_len, 1), 0)
        t = t1 if n_sub == 1 else jnp.concatenate([t1] * n_sub, axis=0)

        def prev(a, k):
            return pltpu.roll(a, k, 0)

        def nxt(a, k):
            return pltpu.roll(a, total - k, 0)

        def window_sum(level):
            s = zp + prev(zp, 1)
            for lv in range(level):
                s = prev(s, 2**lv) + nxt(s, 2**lv)
            if n_sub == 1:
                return s[:sub_len]
            return jnp.concatenate([s[k * stride : k * stride + sub_len] for k in range(n_sub)], axis=0)

        for gi, window in enumerate(POOL_WINDOWS):

            @pl.when(g == gi)
            def _(gi=gi, window=window):
                half = window // 2
                cnt = (jnp.minimum(t - half + window, sub_len) - jnp.maximum(t - half, 0)).astype(F32)
                pooled = window_sum(gi) * (1.0 / cnt) - z
                mixed = jnp.dot(pooled.astype(BF16), w_ref[0].astype(BF16), preferred_element_type=F32)
                o_ref[...] = (mixed * s_ref[...]).astype(o_ref.dtype)

    pl.when(b < n_batch)(lambda: run(rows))
    pl.when(b >= n_batch)(lambda: run(ctx_len))


def pool_mixer(z, pool_w, pool_scale, n_batch, seq, ctx_len):
    M = z.shape[0]
    n_groups, G, _ = pool_w.shape
    assert (M - n_batch * seq) % seq == 0 and seq % ctx_len == 0
    return pl.pallas_call(
        functools.partial(_pool_body, n_batch=n_batch, ctx_len=ctx_len),
        grid=(M // seq, n_groups),
        in_specs=[
            pl.BlockSpec((seq, G), lambda b, g: (b, g)),
            pl.BlockSpec((1, G, G), lambda b, g: (g, 0, 0)),
            pl.BlockSpec((1, G), lambda b, g: (0, g)),
        ],
        out_specs=pl.BlockSpec((seq, G), lambda b, g: (b, g)),
        out_shape=jax.ShapeDtypeStruct((M, n_groups * G), BF16),
        compiler_params=_params(2),
        name="pool_mixer",
    )(z, pool_w, pool_scale.reshape(1, n_groups * G))


def _sgu_body(a_ref, wu_ref, wv_ref, ng_ref, ws_ref, bs_ref, o_ref):
    a = a_ref[...]
    u = jnp.dot(a, wu_ref[...].astype(BF16), preferred_element_type=F32)
    v = jnp.dot(a, wv_ref[...].astype(BF16), preferred_element_type=F32)
    ws = ws_ref[0].astype(BF16)
    bs = bs_ref[0]
    ng = ng_ref[0]
    for c in range(a.shape[0] // SGU_CHUNK):
        rows = slice(c * SGU_CHUNK, (c + 1) * SGU_CHUNK)
        vn = _rms(_gelu_tanh(v[rows]), ng)
        gate = jnp.dot(ws, vn.astype(BF16), preferred_element_type=F32) + bs
        o_ref[rows, :] = (_gelu_tanh(u[rows]) * gate).astype(o_ref.dtype)


def sgu_mixer(h, w_in, wl, norm_g, w_s, b_s, col0, tm=1024):
    M, K = h.shape
    H, P, _ = w_s.shape
    hd = norm_g.shape[1]
    cb0 = col0 // hd
    return pl.pallas_call(
        _sgu_body,
        grid=(M // tm, H),
        in_specs=[
            pl.BlockSpec((tm, K), lambda i, j: (i, 0)),
            pl.BlockSpec((None, K, hd), lambda i, j: (wl, 0, cb0 + j)),
            pl.BlockSpec((None, K, hd), lambda i, j: (wl, 0, cb0 + H + j)),
            pl.BlockSpec((1, 1, hd), lambda i, j: (j, 0, 0)),
            pl.BlockSpec((1, P, P), lambda i, j: (j, 0, 0)),
            pl.BlockSpec((1, P, 1), lambda i, j: (j, 0, 0)),
        ],
        out_specs=pl.BlockSpec((tm, hd), lambda i, j: (i, j)),
        out_shape=jax.ShapeDtypeStruct((M, H * hd), BF16),
        compiler_params=_params(2),
        name="sgu_mixer",
    )(h, w_in, w_in, norm_g.reshape(H, 1, hd), w_s, b_s.reshape(H, P, 1))


def _rope_half(x, cos, sin_signed):
    return x * cos + pltpu.roll(x, LANE // 2, 1) * sin_signed


def _q_in_body(a_ref, w_ref, qag_ref, qng_ref, cos_ref, sin_ref, cq_ref, gq_ref):
    zq = jnp.dot(a_ref[...], w_ref[...], preferred_element_type=F32)
    cq_ref[...] = _rms(zq[:, :MLA_Q_RANK], qag_ref[...]).astype(cq_ref.dtype)
    qs = _exp2_scale(1.0 / math.sqrt(GQA_HEAD_DIM))
    cos = cos_ref[...] * qs
    sin = sin_ref[...] * qs
    g = qng_ref[...]
    for h in range(GQA_HEADS):
        x = _rms(zq[:, MLA_Q_RANK + h * LANE : MLA_Q_RANK + (h + 1) * LANE], g)
        gq_ref[:, h * LANE : (h + 1) * LANE] = _rope_half(x, cos, sin).astype(gq_ref.dtype)


def q_in(h, w_q, m_rows, q_a_g, q_norm_g, cos_g, sin_g, rope_block):
    K, N = w_q.shape
    M = m_rows
    tm = ROPE_TILE
    nq = GQA_HEADS * GQA_HEAD_DIM
    return pl.pallas_call(
        _q_in_body,
        grid=(M // tm,),
        in_specs=[
            pl.BlockSpec((tm, K), lambda i: (i, 0)),
            pl.BlockSpec((K, N), lambda i: (0, 0), pipeline_mode=pl.Buffered(1)),
            pl.BlockSpec((1, MLA_Q_RANK), lambda i: (0, 0)),
            pl.BlockSpec((1, LANE), lambda i: (0, 0)),
            pl.BlockSpec((tm, LANE), rope_block),
            pl.BlockSpec((tm, LANE), rope_block),
        ],
        out_specs=[pl.BlockSpec((tm, MLA_Q_RANK), lambda i: (i, 0)), pl.BlockSpec((tm, nq), lambda i: (i, 0))],
        out_shape=[jax.ShapeDtypeStruct((M, MLA_Q_RANK), BF16), jax.ShapeDtypeStruct((M, nq), BF16)],
        compiler_params=_params(1),
        name="odd_in_q",
    )(h, w_q, q_a_g.reshape(1, -1), q_norm_g.reshape(1, -1), cos_g, sin_g)


def _kv_in_body(a_ref, w_ref, kag_ref, kng_ref, cg_ref, sg_ref, cm_ref, sm_ref, ckv_ref, kr_ref, gk_ref, gv_ref):
    z = jnp.dot(a_ref[...], w_ref[...], preferred_element_type=F32)
    ckv_ref[...] = _rms(z[:, :MLA_KV_RANK], kag_ref[...]).astype(ckv_ref.dtype)
    kr = z[:, MLA_KV_RANK : MLA_KV_RANK + LANE]
    lane = lax.broadcasted_iota(jnp.int32, kr.shape, 1)
    half = MLA_ROPE // 2
    swapped = jnp.where(lane < half, pltpu.roll(kr, LANE - half, 1), pltpu.roll(kr, half, 1))
    kr_ref[...] = (kr * cm_ref[...] + swapped * sm_ref[...]).astype(kr_ref.dtype)
    off_k = MLA_KV_RANK + LANE
    off_v = off_k + GQA_KV_HEADS * LANE
    cos = cg_ref[...]
    sin = sg_ref[...]
    g = kng_ref[...]
    for h in range(GQA_KV_HEADS):
        x = _rms(z[:, off_k + h * LANE : off_k + (h + 1) * LANE], g)
        gk_ref[:, h * LANE : (h + 1) * LANE] = _rope_half(x, cos, sin).astype(gk_ref.dtype)
    gv_ref[...] = z[:, off_v : off_v + GQA_KV_HEADS * LANE].astype(gv_ref.dtype)


def kv_in(h, w_kv, kv_a_g, k_norm_g, cos_g, sin_g, cos_m, sin_m, rope_block):
    M, K = h.shape
    N = w_kv.shape[1]
    tm = ROPE_TILE
    nk = GQA_KV_HEADS * GQA_HEAD_DIM
    tok = lambda i: (i, 0)
    fixed = lambda i: (0, 0)
    return pl.pallas_call(
        _kv_in_body,
        grid=(M // tm,),
        in_specs=[
            pl.BlockSpec((tm, K), tok),
            pl.BlockSpec((K, N), fixed),
            pl.BlockSpec((1, MLA_KV_RANK), fixed),
            pl.BlockSpec((1, LANE), fixed),
            pl.BlockSpec((tm, LANE), rope_block),
            pl.BlockSpec((tm, LANE), rope_block),
            pl.BlockSpec((tm, LANE), rope_block),
            pl.BlockSpec((tm, LANE), rope_block),
        ],
        out_specs=[
            pl.BlockSpec((tm, MLA_KV_RANK), tok),
            pl.BlockSpec((tm, LANE), tok),
            pl.BlockSpec((tm, nk), tok),
            pl.BlockSpec((tm, nk), tok),
        ],
        out_shape=[
            jax.ShapeDtypeStruct((M, MLA_KV_RANK), BF16),
            jax.ShapeDtypeStruct((M, LANE), BF16),
            jax.ShapeDtypeStruct((M, nk), BF16),
            jax.ShapeDtypeStruct((M, nk), BF16),
        ],
        compiler_params=_params(1),
        name="odd_in_kv",
    )(h, w_kv, kv_a_g.reshape(1, -1), k_norm_g.reshape(1, -1), cos_g, sin_g, cos_m, sin_m)


def _mm_qrope_body(a_ref, w_ref, c_ref, s_ref, o_ref):
    acc = jnp.dot(a_ref[...], w_ref[...], preferred_element_type=F32) * _exp2_scale(
        1.0 / math.sqrt(MLA_NOPE + MLA_ROPE)
    )
    cos = c_ref[...]
    sin = s_ref[...]
    for h in range(acc.shape[1] // (2 * LANE)):
        lo = h * 2 * LANE
        o_ref[:, lo : lo + LANE] = acc[:, lo : lo + LANE].astype(o_ref.dtype)
        t = acc[:, lo + LANE : lo + 2 * LANE]
        o_ref[:, lo + LANE : lo + 2 * LANE] = (t * cos + pltpu.roll(t, LANE // 2, 1) * sin).astype(o_ref.dtype)


def matmul_qrope(a, w, cos_m, sin_m_abs, rope_block):
    M, K = a.shape
    tm = ROPE_TILE
    N = tn = w.shape[1]
    return pl.pallas_call(
        _mm_qrope_body,
        grid=(M // tm, N // tn),
        in_specs=[
            pl.BlockSpec((tm, K), lambda i, j: (i, 0)),
            pl.BlockSpec((K, tn), lambda i, j: (0, j)),
            pl.BlockSpec((tm, LANE), rope_block),
            pl.BlockSpec((tm, LANE), rope_block),
        ],
        out_specs=pl.BlockSpec((tm, tn), lambda i, j: (i, j)),
        out_shape=jax.ShapeDtypeStruct((M, N), BF16),
        compiler_params=_params(2),
        name="mla_q_up",
    )(a, w, cos_m, sin_m_abs)


_NT = (((1,), (1,)), ((), ()))
ATTN_SUB = 1024


def _exp2_scale(scale):
    return scale * math.log2(math.e)


def _attend(q_ref, kc, kx, vc, vx, o_ref, s_ref):
    n_sub = q_ref.shape[0] // ATTN_SUB
    n_c = kc.shape[0]
    vc, vx = _with_ones(vc), _with_ones(vx)

    def scores(r):
        slot = r % s_ref.shape[0]
        q = q_ref[pl.ds(r * ATTN_SUB, ATTN_SUB), :]
        s_ref[slot, :, :n_c] = lax.dot_general(q, kc, _NT, preferred_element_type=F32)
        s_ref[slot, :, n_c:] = lax.dot_general(q, kx, _NT, preferred_element_type=F32)
        s = s_ref[slot]
        return s, jnp.max(s, axis=-1, keepdims=True)

    nxt = scores(0)
    for r in range(n_sub):
        s, m = nxt
        p = jnp.exp2(s - m).astype(BF16)
        if r + 1 < n_sub:
            nxt = scores(r + 1)
        o = jnp.dot(p[:, :n_c], vc, preferred_element_type=F32)
        o = o + jnp.dot(p[:, n_c:], vx, preferred_element_type=F32)
        d_v = o.shape[1] - LANE
        inv = 1.0 / o[:, d_v : d_v + 1]
        o_ref[pl.ds(r * ATTN_SUB, ATTN_SUB), :] = (o[:, :d_v] * inv).astype(o_ref.dtype)


def _with_ones(v):
    return jnp.concatenate([v, jnp.ones((v.shape[0], LANE), v.dtype)], axis=1)


def _score_scratch(n_keys):
    return [pltpu.VMEM((2, ATTN_SUB, n_keys), F32)]


def _mla_body(q_ref, kvc_ref, kvx_ref, krc_ref, krx_ref, o_ref, s_ref):
    kc = jnp.concatenate([kvc_ref[:, :LANE], krc_ref[...]], axis=1)
    kx = jnp.concatenate([kvx_ref[:, :LANE], krx_ref[...]], axis=1)
    _attend(q_ref, kc, kx, kvc_ref[:, LANE:], kvx_ref[:, LANE:], o_ref, s_ref)


def mla_attention(q, kv, kr, n_batch, seq, ctx_len):
    cb0 = n_batch * seq // ctx_len
    return pl.pallas_call(
        _mla_body,
        grid=(n_batch, MLA_HEADS),
        in_specs=[
            pl.BlockSpec((seq, 2 * LANE), lambda b, h: (b, h)),
            pl.BlockSpec((ctx_len, 2 * LANE), lambda b, h: (cb0 + b, h)),
            pl.BlockSpec((seq, 2 * LANE), lambda b, h: (b, h)),
            pl.BlockSpec((ctx_len, LANE), lambda b, h: (cb0 + b, 0)),
            pl.BlockSpec((seq, LANE), lambda b, h: (b, 0)),
        ],
        out_specs=pl.BlockSpec((seq, LANE), lambda b, h: (b, h)),
        out_shape=jax.ShapeDtypeStruct((n_batch * seq, MLA_HEADS * MLA_V), BF16),
        scratch_shapes=_score_scratch(ctx_len + seq),
        compiler_params=_params(2),
        name="mla_attention",
    )(q, kv, kv, kr, kr)


def _gqa_body(q_ref, kc_ref, kx_ref, vc_ref, vx_ref, o_ref, s_ref):
    _attend(q_ref, kc_ref[...], kx_ref[...], vc_ref[...], vx_ref[...], o_ref, s_ref)


def gqa_attention(q, k, v, n_batch, seq, ctx_len):
    cb0 = n_batch * seq // ctx_len
    rep = GQA_HEADS // GQA_KV_HEADS
    return pl.pallas_call(
        _gqa_body,
        grid=(n_batch, GQA_HEADS),
        in_specs=[
            pl.BlockSpec((seq, LANE), lambda b, h: (b, h)),
            pl.BlockSpec((ctx_len, LANE), lambda b, h: (cb0 + b, h // rep)),
            pl.BlockSpec((seq, LANE), lambda b, h: (b, h // rep)),
            pl.BlockSpec((ctx_len, LANE), lambda b, h: (cb0 + b, h // rep)),
            pl.BlockSpec((seq, LANE), lambda b, h: (b, h // rep)),
        ],
        out_specs=pl.BlockSpec((seq, LANE), lambda b, h: (b, h)),
        out_shape=jax.ShapeDtypeStruct((n_batch * seq, GQA_HEADS * GQA_HEAD_DIM), BF16),
        scratch_shapes=_score_scratch(ctx_len + seq),
        compiler_params=_params(2),
        name="gqa_attention",
    )(q, k, k, v, v)


ROPE_TILE = 512


def _rope_tables(seq):
    rows = seq // GRID_W
    row = np.repeat(np.arange(rows, dtype=np.float64), GRID_W)
    col = np.tile(np.arange(GRID_W, dtype=np.float64), rows)

    def cos_sin(rot_dim):
        n_freq = rot_dim // 4
        inv = ROPE_THETA ** (-np.arange(n_freq, dtype=np.float64) / n_freq)
        ang = np.concatenate([row[:, None] * inv, col[:, None] * inv], axis=-1)
        return np.cos(ang), np.sin(ang)

    def table(x_part, ident_row):
        ident = np.broadcast_to(ident_row, (ROPE_TILE, LANE))
        return jnp.asarray(np.concatenate([x_part, ident], axis=0), F32)

    ones, zeros = np.ones((1, LANE)), np.zeros((1, LANE))
    cg, sg = cos_sin(GQA_HEAD_DIM)
    cos_g = table(np.concatenate([cg, cg], axis=1), ones)
    sin_g = table(np.concatenate([-sg, sg], axis=1), zeros)
    cm, sm = cos_sin(MLA_ROPE)
    zpad = np.zeros((seq, LANE - MLA_ROPE))
    ident_m = np.concatenate([np.ones((1, MLA_ROPE)), np.zeros((1, LANE - MLA_ROPE))], axis=1)
    cos_m = table(np.concatenate([cm, cm, zpad], axis=1), ident_m)
    sin_m = table(np.concatenate([-sm, sm, zpad], axis=1), zeros)
    sin_m_abs = table(np.concatenate([sm, sm, zpad], axis=1), zeros)
    return cos_g, sin_g, cos_m, sin_m, sin_m_abs


def _rope_block(seq, n_latent_rows):
    per_seq = seq // ROPE_TILE
    n_latent = n_latent_rows // ROPE_TILE
    return lambda i, *_: (jnp.where(i < n_latent, i % per_seq, per_seq), 0)


def _mla_q_weight(w_q_b):
    r = w_q_b.shape[0]
    w = w_q_b.reshape(r, MLA_HEADS, MLA_NOPE + MLA_ROPE)
    nope, rope = w[..., :MLA_NOPE], w[..., MLA_NOPE:]
    x1, x2 = rope[..., : MLA_ROPE // 2], rope[..., MLA_ROPE // 2 :]
    return jnp.concatenate([nope, rope, -x2, x1], axis=-1).reshape(r, MLA_HEADS * 2 * LANE).astype(BF16)


def _odd_kv_weight(w_in, q_cols):
    w = w_in[:, q_cols:]
    pad = jnp.zeros((w.shape[0], LANE - MLA_ROPE), w.dtype)
    split = MLA_KV_RANK + MLA_ROPE
    return jnp.concatenate([w[:, :split], pad, w[:, split:]], axis=1).astype(BF16)


def kernel(x, c, ctx, c_ctx, mod_w, mod_b, norm1_g, norm2_g, final_g, ffn_w_gu, ffn_w_down, ev_w_in, ev_pool_w,
           ev_pool_scale, ev_sgu_norm_g, ev_sgu_w_s, ev_sgu_b, ev_w_out, od_w_in, od_q_a_g, od_w_q_b, od_kv_a_g,
           od_w_kv_b, od_q_norm_g, od_k_norm_g, od_w_out):
    B, S, D = x.shape
    CL = ctx.shape[1]
    depth = mod_w.shape[0]
    assert depth == 2, "layer schedule below is written for one even and one odd layer"
    n_x = B * S
    n_all = n_x + B * CL
    x_rows = x.reshape(n_x, D)
    c_rows = ctx.reshape(B * CL, D)

    cc = jnp.concatenate([c, c_ctx[None, :], jnp.zeros((MOD_ROWS - B - 1, D), F32)], axis=0)
    mods = Mods(adaln_all(cc, mod_w, mod_b), B, S)

    def ffn(t_in, layer, m_rows):
        h = norm_mod([t_in], norm2_g[layer], mods, layer, SH2, SC2)
        act, w_down = matmul_swiglu(h, ffn_w_gu, ffn_w_down, layer, m_rows)
        return matmul_residual([act], w_down, 0, [t_in], mods, layer, G2, m_rows, tm=512, tn=512,
                               name="ffn_down")

    pool_width = ev_pool_scale.shape[1]
    h = norm_mod([x_rows, c_rows], norm1_g[0], mods, 0, SH1, SC1)
    w_in_ev = ev_w_in.astype(BF16)
    z = matmul(h, w_in_ev, 0, n_all, pool_width, F32, name="even_in")
    ya = pool_mixer(z, ev_pool_w[0], ev_pool_scale[0], B, S, CL)
    yb = sgu_mixer(h, w_in_ev, 0, ev_sgu_norm_g[0], ev_sgu_w_s[0], ev_sgu_b[0], pool_width)
    t = matmul_residual([ya, yb], ev_w_out.astype(BF16), 0, [x_rows, c_rows], mods, 0, G1, n_all,
                        name="even_out")
    t = ffn(t, 0, n_all)

    q_cols = MLA_Q_RANK + GQA_HEADS * GQA_HEAD_DIM
    cos_g, sin_g, cos_m, sin_m, sin_m_abs = _rope_tables(S)
    rope_block = _rope_block(S, n_x)
    h = norm_mod([t], norm1_g[1], mods, 1, SH1, SC1)
    w_in_bf = od_w_in[0].astype(BF16)
    cq, gq = q_in(h, w_in_bf[:, :q_cols], n_x, od_q_a_g[0], od_q_norm_g[0], cos_g, sin_g, rope_block)
    ckv, kr, gk, gv = kv_in(h, _odd_kv_weight(w_in_bf, q_cols), od_kv_a_g[0], od_k_norm_g[0], cos_g, sin_g,
                            cos_m, sin_m, rope_block)
    q_m = matmul_qrope(cq, _mla_q_weight(od_w_q_b[0]), cos_m, sin_m_abs, rope_block)
    kv_m = matmul(ckv, od_w_kv_b, 0, n_all, od_w_kv_b.shape[2], BF16, tm=512, tn=od_w_kv_b.shape[2],
                  name="mla_kv_up")
    o_m = mla_attention(q_m, kv_m, kr, B, S, CL)
    o_g = gqa_attention(gq, gk, gv, B, S, CL)
    t = matmul_residual([o_m, o_g], od_w_out.astype(BF16), 0, [t], mods, 1, G1, n_x, name="odd_out")
    t = ffn(t, 1, n_x)

    return final_norm(t, final_g, n_x).reshape(B, S, D)
```

```python
import functools
import math

import jax
import jax.numpy as jnp
import numpy as np
from jax import lax
from jax.experimental import pallas as pl
from jax.experimental.pallas import tpu as pltpu

F32 = jnp.float32
BF16 = jnp.bfloat16

GRID_W = 64
EPS = 1e-6
ROPE_THETA = 10000.0
POOL_WINDOWS = (2, 4, 8, 16)
POOL_HALO = 16
SGU_CHUNK = 128
MLA_HEADS = 16
MLA_Q_RANK = 1024
MLA_KV_RANK = 512
MLA_NOPE = 128
MLA_ROPE = 64
MLA_V = 128
GQA_HEADS = 16
GQA_KV_HEADS = 4
GQA_HEAD_DIM = 128
LANE = 128
MOD_ROWS = 16
SH1, SC1, G1, SH2, SC2, G2 = range(6)

VMEM_LIMIT = 56 * 1024 * 1024


def _params(n_axes):
    return pltpu.CompilerParams(dimension_semantics=("arbitrary",) * n_axes, vmem_limit_bytes=VMEM_LIMIT)


def _silu(x):
    return x / (1.0 + jnp.exp(-x))


def _gelu_tanh(x):
    c = math.sqrt(2.0 / math.pi)
    return 0.5 * x * (1.0 + jnp.tanh(x * (c + (c * 0.044715) * (x * x))))


def _rms(x, g):
    return x * lax.rsqrt(jnp.mean(x * x, axis=-1, keepdims=True) + EPS) * g


def _row_sources(srcs, tm, width, col_of):
    n0 = srcs[0].shape[0] // tm
    if len(srcs) == 1:
        return n0, [pl.BlockSpec((tm, width), lambda i, *r: (i, col_of(*r)))]
    n1 = srcs[1].shape[0] // tm
    return n0, [
        pl.BlockSpec((tm, width), lambda i, *r: (jnp.minimum(i, n0 - 1), col_of(*r))),
        pl.BlockSpec((tm, width), lambda i, *r: (jnp.clip(i - n0, 0, n1 - 1), col_of(*r))),
    ]


def _from_source(refs, n0, i):
    if len(refs) == 1:
        return refs[0][...]
    return jnp.where(i < n0, refs[0][...], refs[1][...])


def _adaln_body(c_ref, w_ref, b_ref, o_ref):
    sc = _silu(c_ref[...]).astype(BF16)
    m = jnp.dot(sc, w_ref[0].astype(BF16), preferred_element_type=F32) + b_ref[0]
    for r in range(MOD_ROWS):
        o_ref[r] = m[r : r + 1]


def adaln_all(cc, mod_w, mod_b, tn=1024):
    L, D, N = mod_w.shape
    per_chunk = D // tn
    return pl.pallas_call(
        _adaln_body,
        grid=(L, N // tn),
        in_specs=[
            pl.BlockSpec((MOD_ROWS, D), lambda l, j: (0, 0)),
            pl.BlockSpec((1, D, tn), lambda l, j: (l, 0, j)),
            pl.BlockSpec((1, 1, tn), lambda l, j: (l, 0, j)),
        ],
        out_specs=pl.BlockSpec((MOD_ROWS, 1, tn), lambda l, j: (l * (N // D) + j // per_chunk, 0, j % per_chunk)),
        out_shape=jax.ShapeDtypeStruct((L * (N // D) * MOD_ROWS, 1, D), F32),
        compiler_params=_params(2),
        name="adaln",
    )(cc, mod_w, mod_b.reshape(L, 1, N))


class Mods:
    def __init__(self, table, n_batch, seq):
        self.table = table
        self.n_batch = n_batch
        self.seq = seq

    def index(self, layer, which, tm):
        n_x = self.n_batch * self.seq // tm
        base = (layer * 6 + which) * MOD_ROWS
        return lambda i: base + jnp.where(i < n_x, (i * tm) // self.seq, self.n_batch)


NORM_ROW_CHUNK = 256


def _norm_mod_body(*refs, n_src, n0):
    g_ref, sh_ref, sc_ref, o_ref = refs[n_src:]

    def emit(x_ref):
        gs = g_ref[...] * (1.0 + sc_ref[0])
        sh = sh_ref[0]
        for c in range(x_ref.shape[0] // NORM_ROW_CHUNK):
            rows = pl.ds(c * NORM_ROW_CHUNK, NORM_ROW_CHUNK)
            o_ref[rows, :] = (_rms(x_ref[rows, :], gs) + sh).astype(o_ref.dtype)

    if n_src == 1:
        emit(refs[0])
    else:
        i = pl.program_id(0)
        pl.when(i < n0)(lambda: emit(refs[0]))
        pl.when(i >= n0)(lambda: emit(refs[1]))


def norm_mod(srcs, gain, mods, layer, which_shift, which_scale, tm=512):
    D = srcs[0].shape[1]
    M = sum(s.shape[0] for s in srcs)
    sh_idx = mods.index(layer, which_shift, tm)
    sc_idx = mods.index(layer, which_scale, tm)
    n0, src_specs = _row_sources(srcs, tm, D, lambda: 0)
    return pl.pallas_call(
        functools.partial(_norm_mod_body, n_src=len(srcs), n0=n0),
        grid=(M // tm,),
        in_specs=src_specs
        + [
            pl.BlockSpec((1, D), lambda i: (0, 0)),
            pl.BlockSpec((1, 1, D), lambda i: (sh_idx(i), 0, 0)),
            pl.BlockSpec((1, 1, D), lambda i: (sc_idx(i), 0, 0)),
        ],
        out_specs=pl.BlockSpec((tm, D), lambda i: (i, 0)),
        out_shape=jax.ShapeDtypeStruct((M, D), BF16),
        compiler_params=_params(1),
        name="norm_mod",
    )(*srcs, gain.reshape(1, D), mods.table, mods.table)


def _rms_body(x_ref, g_ref, o_ref):
    o_ref[...] = _rms(x_ref[...], g_ref[...]).astype(o_ref.dtype)


def final_norm(t, gain, m_rows, tm=512):
    D = t.shape[1]
    return pl.pallas_call(
        _rms_body,
        grid=(m_rows // tm,),
        in_specs=[pl.BlockSpec((tm, D), lambda i: (i, 0)), pl.BlockSpec((1, D), lambda i: (0, 0))],
        out_specs=pl.BlockSpec((tm, D), lambda i: (i, 0)),
        out_shape=jax.ShapeDtypeStruct((m_rows, D), F32),
        compiler_params=_params(1),
        name="final_norm",
    )(t, gain.reshape(1, D))


def _mm_body(a_ref, w_ref, o_ref):
    o_ref[...] = jnp.dot(a_ref[...], w_ref[...].astype(BF16), preferred_element_type=F32).astype(o_ref.dtype)


def matmul(a, w, wl, m_rows, n_cols, out_dtype, tm=1024, tn=512, name="mm"):
    K = w.shape[1]
    return pl.pallas_call(
        _mm_body,
        grid=(m_rows // tm, n_cols // tn),
        in_specs=[
            pl.BlockSpec((tm, K), lambda i, j: (i, 0)),
            pl.BlockSpec((None, K, tn), lambda i, j: (wl, 0, j)),
        ],
        out_specs=pl.BlockSpec((tm, tn), lambda i, j: (i, j)),
        out_shape=jax.ShapeDtypeStruct((m_rows, n_cols), out_dtype),
        compiler_params=_params(2),
        name=name,
    )(a, w)


def _mm_res_body(*refs, n_a, n_res, n0):
    a_refs, w_refs = refs[:n_a], refs[n_a : 2 * n_a]
    r_refs = refs[2 * n_a : 2 * n_a + n_res]
    g_ref, o_ref = refs[2 * n_a + n_res :]
    acc = jnp.dot(a_refs[0][...], w_refs[0][...].astype(BF16), preferred_element_type=F32)
    for a_ref, w_ref in zip(a_refs[1:], w_refs[1:]):
        acc = acc + jnp.dot(a_ref[...], w_ref[...].astype(BF16), preferred_element_type=F32)
    o_ref[...] = _from_source(r_refs, n0, pl.program_id(0)) + g_ref[0] * acc


def matmul_residual(a_parts, w, wl, res_srcs, mods, layer, which_gate, m_rows, tm=1024, tn=512, name="mm_res"):
    K, N = w.shape[1:]
    kp = K // len(a_parts)
    g_idx = mods.index(layer, which_gate, tm)
    n0, res_specs = _row_sources(res_srcs, tm, tn, lambda j: j)
    a_specs = [pl.BlockSpec((tm, kp), lambda i, j: (i, 0)) for _ in a_parts]
    w_specs = [pl.BlockSpec((None, kp, tn), lambda i, j, p=p: (wl, p, j)) for p in range(len(a_parts))]
    return pl.pallas_call(
        functools.partial(_mm_res_body, n_a=len(a_parts), n_res=len(res_srcs), n0=n0),
        grid=(m_rows // tm, N // tn),
        in_specs=a_specs + w_specs + res_specs + [pl.BlockSpec((1, 1, tn), lambda i, j: (g_idx(i), 0, j))],
        out_specs=pl.BlockSpec((tm, tn), lambda i, j: (i, j)),
        out_shape=jax.ShapeDtypeStruct((m_rows, N), F32),
        compiler_params=_params(2),
        name=name,
    )(*a_parts, *([w] * len(a_parts)), *res_srcs, mods.table)


def _swiglu_body(a_ref, wg_ref, wu_ref, wd_ref, o_ref, wd_bf_ref):
    a = a_ref[...]
    g = jnp.dot(a, wg_ref[...].astype(BF16), preferred_element_type=F32)
    u = jnp.dot(a, wu_ref[...].astype(BF16), preferred_element_type=F32)
    o_ref[...] = (_silu(g) * u).astype(o_ref.dtype)
    @pl.when(pl.program_id(0) == 0)
    def _():
        wd_bf_ref[...] = wd_ref[...].astype(BF16)


def matmul_swiglu(a, w_gu, w_down, wl, m_rows, tm=1024, tn=256):
    K, N2 = w_gu.shape[1:]
    H, Nd = w_down.shape[1:]
    assert N2 == 2 * H
    nj = H // tn
    wd_block = lambda i, j: jnp.where(i == 0, j, nj - 1)
    return pl.pallas_call(
        _swiglu_body,
        grid=(m_rows // tm, nj),
        in_specs=[
            pl.BlockSpec((tm, K), lambda i, j: (i, 0)),
            pl.BlockSpec((None, K, tn), lambda i, j: (wl, 0, j)),
            pl.BlockSpec((None, K, tn), lambda i, j: (wl, 0, j + nj)),
            pl.BlockSpec((None, tn, Nd), lambda i, j: (wl, wd_block(i, j), 0)),
        ],
        out_specs=[
            pl.BlockSpec((tm, tn), lambda i, j: (i, j)),
            pl.BlockSpec((None, tn, Nd), lambda i, j: (0, wd_block(i, j), 0)),
        ],
        out_shape=[jax.ShapeDtypeStruct((m_rows, H), BF16), jax.ShapeDtypeStruct((1, H, Nd), BF16)],
        compiler_params=_params(2),
        name="ffn_up",
    )(a, w_gu, w_gu, w_down)


def _pool_body(z_ref, w_ref, s_ref, o_ref, *, n_batch, ctx_len):
    b = pl.program_id(0)
    g = pl.program_id(1)
    rows, C = z_ref.shape

    def run(sub_len):
        n_sub = rows // sub_len
        stride = sub_len + POOL_HALO
        z = z_ref[...]
        gap = jnp.zeros((POOL_HALO, C), F32)
        zp = jnp.concatenate([p for k in range(n_sub) for p in (z[k * sub_len : (k + 1) * sub_len], gap)], axis=0)
        total = n_sub * stride
        t1 = lax.broadcasted_iota(jnp.int32, (sub_len, 1), 0)
        t = t1 if n_sub == 1 else jnp.concatenate([t1] * n_sub, axis=0)

        def prev(a, k):
            return pltpu.roll(a, k, 0)

        def nxt(a, k):
            return pltpu.roll(a, total - k, 0)

        def window_sum(level):
            s = zp + prev(zp, 1)
            for lv in range(level):
                s = prev(s, 2**lv) + nxt(s, 2**lv)
            if n_sub == 1:
                return s[:sub_len]
            return jnp.concatenate([s[k * stride : k * stride + sub_len] for k in range(n_sub)], axis=0)

        for gi, window in enumerate(POOL_WINDOWS):

            @pl.when(g == gi)
            def _(gi=gi, window=window):
                half = window // 2
                cnt = (jnp.minimum(t - half + window, sub_len) - jnp.maximum(t - half, 0)).astype(F32)
                pooled = window_sum(gi) * (1.0 / cnt) - z
                mixed = jnp.dot(pooled.astype(BF16), w_ref[0].astype(BF16), preferred_element_type=F32)
                o_ref[...] = (mixed * s_ref[...]).astype(o_ref.dtype)

    pl.when(b < n_batch)(lambda: run(rows))
    pl.when(b >= n_batch)(lambda: run(ctx_len))


def pool_mixer(z, pool_w, pool_scale, n_batch, seq, ctx_len):
    M = z.shape[0]
    n_groups, G, _ = pool_w.shape
    assert (M - n_batch * seq) % seq == 0 and seq % ctx_len == 0
    return pl.pallas_call(
        functools.partial(_pool_body, n_batch=n_batch, ctx_len=ctx_len),
        grid=(M // seq, n_groups),
        in_specs=[
            pl.BlockSpec((seq, G), lambda b, g: (b, g)),
            pl.BlockSpec((1, G, G), lambda b, g: (g, 0, 0)),
            pl.BlockSpec((1, G), lambda b, g: (0, g)),
        ],
        out_specs=pl.BlockSpec((seq, G), lambda b, g: (b, g)),
        out_shape=jax.ShapeDtypeStruct((M, n_groups * G), BF16),
        compiler_params=_params(2),
        name="pool_mixer",
    )(z, pool_w, pool_scale.reshape(1, n_groups * G))


def _sgu_body(a_ref, wu_ref, wv_ref, ng_ref, ws_ref, bs_ref, o_ref):
    a = a_ref[...]
    u = jnp.dot(a, wu_ref[...].astype(BF16), preferred_element_type=F32)
    v = jnp.dot(a, wv_ref[...].astype(BF16), preferred_element_type=F32)
    ws = ws_ref[0].astype(BF16)
    bs = bs_ref[0]
    ng = ng_ref[0]
    for c in range(a.shape[0] // SGU_CHUNK):
        rows = slice(c * SGU_CHUNK, (c + 1) * SGU_CHUNK)
        vn = _rms(_gelu_tanh(v[rows]), ng)
        gate = jnp.dot(ws, vn.astype(BF16), preferred_element_type=F32) + bs
        o_ref[rows, :] = (_gelu_tanh(u[rows]) * gate).astype(o_ref.dtype)


def sgu_mixer(h, w_in, wl, norm_g, w_s, b_s, col0, tm=1024):
    M, K = h.shape
    H, P, _ = w_s.shape
    hd = norm_g.shape[1]
    cb0 = col0 // hd
    return pl.pallas_call(
        _sgu_body,
        grid=(M // tm, H),
        in_specs=[
            pl.BlockSpec((tm, K), lambda i, j: (i, 0)),
            pl.BlockSpec((None, K, hd), lambda i, j: (wl, 0, cb0 + j)),
            pl.BlockSpec((None, K, hd), lambda i, j: (wl, 0, cb0 + H + j)),
            pl.BlockSpec((1, 1, hd), lambda i, j: (j, 0, 0)),
            pl.BlockSpec((1, P, P), lambda i, j: (j, 0, 0)),
            pl.BlockSpec((1, P, 1), lambda i, j: (j, 0, 0)),
        ],
        out_specs=pl.BlockSpec((tm, hd), lambda i, j: (i, j)),
        out_shape=jax.ShapeDtypeStruct((M, H * hd), BF16),
        compiler_params=_params(2),
        name="sgu_mixer",
    )(h, w_in, w_in, norm_g.reshape(H, 1, hd), w_s, b_s.reshape(H, P, 1))


def _rope_half(x, cos, sin_signed):
    return x * cos + pltpu.roll(x, LANE // 2, 1) * sin_signed


def _q_in_body(a_ref, w_ref, qag_ref, qng_ref, cos_ref, sin_ref, cq_ref, gq_ref):
    zq = jnp.dot(a_ref[...], w_ref[...], preferred_element_type=F32)
    cq_ref[...] = _rms(zq[:, :MLA_Q_RANK], qag_ref[...]).astype(cq_ref.dtype)
    qs = _exp2_scale(1.0 / math.sqrt(GQA_HEAD_DIM))
    cos = cos_ref[...] * qs
    sin = sin_ref[...] * qs
    g = qng_ref[...]
    for h in range(GQA_HEADS):
        x = _rms(zq[:, MLA_Q_RANK + h * LANE : MLA_Q_RANK + (h + 1) * LANE], g)
        gq_ref[:, h * LANE : (h + 1) * LANE] = _rope_half(x, cos, sin).astype(gq_ref.dtype)


def q_in(h, w_q, m_rows, q_a_g, q_norm_g, cos_g, sin_g, rope_block):
    K, N = w_q.shape
    M = m_rows
    tm = ROPE_TILE
    nq = GQA_HEADS * GQA_HEAD_DIM
    return pl.pallas_call(
        _q_in_body,
        grid=(M // tm,),
        in_specs=[
            pl.BlockSpec((tm, K), lambda i: (i, 0)),
            pl.BlockSpec((K, N), lambda i: (0, 0), pipeline_mode=pl.Buffered(1)),
            pl.BlockSpec((1, MLA_Q_RANK), lambda i: (0, 0)),
            pl.BlockSpec((1, LANE), lambda i: (0, 0)),
            pl.BlockSpec((tm, LANE), rope_block),
            pl.BlockSpec((tm, LANE), rope_block),
        ],
        out_specs=[pl.BlockSpec((tm, MLA_Q_RANK), lambda i: (i, 0)), pl.BlockSpec((tm, nq), lambda i: (i, 0))],
        out_shape=[jax.ShapeDtypeStruct((M, MLA_Q_RANK), BF16), jax.ShapeDtypeStruct((M, nq), BF16)],
        compiler_params=_params(1),
        name="odd_in_q",
    )(h, w_q, q_a_g.reshape(1, -1), q_norm_g.reshape(1, -1), cos_g, sin_g)


def _kv_in_body(a_ref, w_ref, kag_ref, kng_ref, cg_ref, sg_ref, cm_ref, sm_ref, ckv_ref, kr_ref, gk_ref, gv_ref):
    z = jnp.dot(a_ref[...], w_ref[...], preferred_element_type=F32)
    ckv_ref[...] = _rms(z[:, :MLA_KV_RANK], kag_ref[...]).astype(ckv_ref.dtype)
    kr = z[:, MLA_KV_RANK : MLA_KV_RANK + LANE]
    lane = lax.broadcasted_iota(jnp.int32, kr.shape, 1)
    half = MLA_ROPE // 2
    swapped = jnp.where(lane < half, pltpu.roll(kr, LANE - half, 1), pltpu.roll(kr, half, 1))
    kr_ref[...] = (kr * cm_ref[...] + swapped * sm_ref[...]).astype(kr_ref.dtype)
    off_k = MLA_KV_RANK + LANE
    off_v = off_k + GQA_KV_HEADS * LANE
    cos = cg_ref[...]
    sin = sg_ref[...]
    g = kng_ref[...]
    for h in range(GQA_KV_HEADS):
        x = _rms(z[:, off_k + h * LANE : off_k + (h + 1) * LANE], g)
        gk_ref[:, h * LANE : (h + 1) * LANE] = _rope_half(x, cos, sin).astype(gk_ref.dtype)
    gv_ref[...] = z[:, off_v : off_v + GQA_KV_HEADS * LANE].astype(gv_ref.dtype)


def kv_in(h, w_kv, kv_a_g, k_norm_g, cos_g, sin_g, cos_m, sin_m, rope_block):
    M, K = h.shape
    N = w_kv.shape[1]
    tm = ROPE_TILE
    nk = GQA_KV_HEADS * GQA_HEAD_DIM
    tok = lambda i: (i, 0)
    fixed = lambda i: (0, 0)
    return pl.pallas_call(
        _kv_in_body,
        grid=(M // tm,),
        in_specs=[
            pl.BlockSpec((tm, K), tok),
            pl.BlockSpec((K, N), fixed),
            pl.BlockSpec((1, MLA_KV_RANK), fixed),
            pl.BlockSpec((1, LANE), fixed),
            pl.BlockSpec((tm, LANE), rope_block),
            pl.BlockSpec((tm, LANE), rope_block),
            pl.BlockSpec((tm, LANE), rope_block),
            pl.BlockSpec((tm, LANE), rope_block),
        ],
        out_specs=[
            pl.BlockSpec((tm, MLA_KV_RANK), tok),
            pl.BlockSpec((tm, LANE), tok),
            pl.BlockSpec((tm, nk), tok),
            pl.BlockSpec((tm, nk), tok),
        ],
        out_shape=[
            jax.ShapeDtypeStruct((M, MLA_KV_RANK), BF16),
            jax.ShapeDtypeStruct((M, LANE), BF16),
            jax.ShapeDtypeStruct((M, nk), BF16),
            jax.ShapeDtypeStruct((M, nk), BF16),
        ],
        compiler_params=_params(1),
        name="odd_in_kv",
    )(h, w_kv, kv_a_g.reshape(1, -1), k_norm_g.reshape(1, -1), cos_g, sin_g, cos_m, sin_m)


def _mm_qrope_body(a_ref, w_ref, c_ref, s_ref, o_ref):
    acc = jnp.dot(a_ref[...], w_ref[...], preferred_element_type=F32) * _exp2_scale(
        1.0 / math.sqrt(MLA_NOPE + MLA_ROPE)
    )
    cos = c_ref[...]
    sin = s_ref[...]
    for h in range(acc.shape[1] // (2 * LANE)):
        lo = h * 2 * LANE
        o_ref[:, lo : lo + LANE] = acc[:, lo : lo + LANE].astype(o_ref.dtype)
        t = acc[:, lo + LANE : lo + 2 * LANE]
        o_ref[:, lo + LANE : lo + 2 * LANE] = (t * cos + pltpu.roll(t, LANE // 2, 1) * sin).astype(o_ref.dtype)


def matmul_qrope(a, w, cos_m, sin_m_abs, rope_block):
    M, K = a.shape
    tm = ROPE_TILE
    N = tn = w.shape[1]
    return pl.pallas_call(
        _mm_qrope_body,
        grid=(M // tm, N // tn),
        in_specs=[
            pl.BlockSpec((tm, K), lambda i, j: (i, 0)),
            pl.BlockSpec((K, tn), lambda i, j: (0, j)),
            pl.BlockSpec((tm, LANE), rope_block),
            pl.BlockSpec((tm, LANE), rope_block),
        ],
        out_specs=pl.BlockSpec((tm, tn), lambda i, j: (i, j)),
        out_shape=jax.ShapeDtypeStruct((M, N), BF16),
        compiler_params=_params(2),
        name="mla_q_up",
    )(a, w, cos_m, sin_m_abs)


_NT = (((1,), (1,)), ((), ()))
ATTN_SUB = 1024


def _exp2_scale(scale):
    return scale * math.log2(math.e)


def _attend(q_ref, kc, kx, vc, vx, o_ref, s_ref):
    n_sub = q_ref.shape[0] // ATTN_SUB
    n_c = kc.shape[0]
    vc, vx = _with_ones(vc), _with_ones(vx)

    def scores(r):
        slot = r % s_ref.shape[0]
        q = q_ref[pl.ds(r * ATTN_SUB, ATTN_SUB), :]
        s_ref[slot, :, :n_c] = lax.dot_general(q, kc, _NT, preferred_element_type=F32)
        s_ref[slot, :, n_c:] = lax.dot_general(q, kx, _NT, preferred_element_type=F32)
        s = s_ref[slot]
        return s, jnp.max(s, axis=-1, keepdims=True)

    nxt = scores(0)
    for r in range(n_sub):
        s, m = nxt
        p = jnp.exp2(s - m).astype(BF16)
        if r + 1 < n_sub:
            nxt = scores(r + 1)
        o = jnp.dot(p[:, :n_c], vc, preferred_element_type=F32)
        o = o + jnp.dot(p[:, n_c:], vx, preferred_element_type=F32)
        d_v = o.shape[1] - LANE
        inv = 1.0 / o[:, d_v : d_v + 1]
        o_ref[pl.ds(r * ATTN_SUB, ATTN_SUB), :] = (o[:, :d_v] * inv).astype(o_ref.dtype)


def _with_ones(v):
    return jnp.concatenate([v, jnp.ones((v.shape[0], LANE), v.dtype)], axis=1)


def _score_scratch(n_keys):
    return [pltpu.VMEM((2, ATTN_SUB, n_keys), F32)]


def _mla_body(q_ref, kvc_ref, kvx_ref, krc_ref, krx_ref, o_ref, s_ref):
    kc = jnp.concatenate([kvc_ref[:, :LANE], krc_ref[...]], axis=1)
    kx = jnp.concatenate([kvx_ref[:, :LANE], krx_ref[...]], axis=1)
    _attend(q_ref, kc, kx, kvc_ref[:, LANE:], kvx_ref[:, LANE:], o_ref, s_ref)


def mla_attention(q, kv, kr, n_batch, seq, ctx_len):
    cb0 = n_batch * seq // ctx_len
    return pl.pallas_call(
        _mla_body,
        grid=(n_batch, MLA_HEADS),
        in_specs=[
            pl.BlockSpec((seq, 2 * LANE), lambda b, h: (b, h)),
            pl.BlockSpec((ctx_len, 2 * LANE), lambda b, h: (cb0 + b, h)),
            pl.BlockSpec((seq, 2 * LANE), lambda b, h: (b, h)),
            pl.BlockSpec((ctx_len, LANE), lambda b, h: (cb0 + b, 0)),
            pl.BlockSpec((seq, LANE), lambda b, h: (b, 0)),
        ],
        out_specs=pl.BlockSpec((seq, LANE), lambda b, h: (b, h)),
        out_shape=jax.ShapeDtypeStruct((n_batch * seq, MLA_HEADS * MLA_V), BF16),
        scratch_shapes=_score_scratch(ctx_len + seq),
        compiler_params=_params(2),
        name="mla_attention",
    )(q, kv, kv, kr, kr)


def _gqa_body(q_ref, kc_ref, kx_ref, vc_ref, vx_ref, o_ref, s_ref):
    _attend(q_ref, kc_ref[...], kx_ref[...], vc_ref[...], vx_ref[...], o_ref, s_ref)


def gqa_attention(q, k, v, n_batch, seq, ctx_len):
    cb0 = n_batch * seq // ctx_len
    rep = GQA_HEADS // GQA_KV_HEADS
    return pl.pallas_call(
        _gqa_body,
        grid=(n_batch, GQA_HEADS),
        in_specs=[
            pl.BlockSpec((seq, LANE), lambda b, h: (b, h)),
            pl.BlockSpec((ctx_len, LANE), lambda b, h: (cb0 + b, h // rep)),
            pl.BlockSpec((seq, LANE), lambda b, h: (b, h // rep)),
            pl.BlockSpec((ctx_len, LANE), lambda b, h: (cb0 + b, h // rep)),
            pl.BlockSpec((seq, LANE), lambda b, h: (b, h // rep)),
        ],
        out_specs=pl.BlockSpec((seq, LANE), lambda b, h: (b, h)),
        out_shape=jax.ShapeDtypeStruct((n_batch * seq, GQA_HEADS * GQA_HEAD_DIM), BF16),
        scratch_shapes=_score_scratch(ctx_len + seq),
        compiler_params=_params(2),
        name="gqa_attention",
    )(q, k, k, v, v)


ROPE_TILE = 512


def _rope_tables(seq):
    rows = seq // GRID_W
    row = np.repeat(np.arange(rows, dtype=np.float64), GRID_W)
    col = np.tile(np.arange(GRID_W, dtype=np.float64), rows)

    def cos_sin(rot_dim):
        n_freq = rot_dim // 4
        inv = ROPE_THETA ** (-np.arange(n_freq, dtype=np.float64) / n_freq)
        ang = np.concatenate([row[:, None] * inv, col[:, None] * inv], axis=-1)
        return np.cos(ang), np.sin(ang)

    def table(x_part, ident_row):
        ident = np.broadcast_to(ident_row, (ROPE_TILE, LANE))
        return jnp.asarray(np.concatenate([x_part, ident], axis=0), F32)

    ones, zeros = np.ones((1, LANE)), np.zeros((1, LANE))
    cg, sg = cos_sin(GQA_HEAD_DIM)
    cos_g = table(np.concatenate([cg, cg], axis=1), ones)
    sin_g = table(np.concatenate([-sg, sg], axis=1), zeros)
    cm, sm = cos_sin(MLA_ROPE)
    zpad = np.zeros((seq, LANE - MLA_ROPE))
    ident_m = np.concatenate([np.ones((1, MLA_ROPE)), np.zeros((1, LANE - MLA_ROPE))], axis=1)
    cos_m = table(np.concatenate([cm, cm, zpad], axis=1), ident_m)
    sin_m = table(np.concatenate([-sm, sm, zpad], axis=1), zeros)
    sin_m_abs = table(np.concatenate([sm, sm, zpad], axis=1), zeros)
    return cos_g, sin_g, cos_m, sin_m, sin_m_abs


def _rope_block(seq, n_latent_rows):
    per_seq = seq // ROPE_TILE
    n_latent = n_latent_rows // ROPE_TILE
    return lambda i, *_: (jnp.where(i < n_latent, i % per_seq, per_seq), 0)


def _mla_q_weight(w_q_b):
    r = w_q_b.shape[0]
    w = w_q_b.reshape(r, MLA_HEADS, MLA_NOPE + MLA_ROPE)
    nope, rope = w[..., :MLA_NOPE], w[..., MLA_NOPE:]
    x1, x2 = rope[..., : MLA_ROPE // 2], rope[..., MLA_ROPE // 2 :]
    return jnp.concatenate([nope, rope, -x2, x1], axis=-1).reshape(r, MLA_HEADS * 2 * LANE).astype(BF16)


def _odd_kv_weight(w_in, q_cols):
    w = w_in[:, q_cols:]
    pad = jnp.zeros((w.shape[0], LANE - MLA_ROPE), w.dtype)
    split = MLA_KV_RANK + MLA_ROPE
    return jnp.concatenate([w[:, :split], pad, w[:, split:]], axis=1).astype(BF16)


def kernel(x, c, ctx, c_ctx, mod_w, mod_b, norm1_g, norm2_g, final_g, ffn_w_gu, ffn_w_down, ev_w_in, ev_pool_w,
           ev_pool_scale, ev_sgu_norm_g, ev_sgu_w_s, ev_sgu_b, ev_w_out, od_w_in, od_q_a_g, od_w_q_b, od_kv_a_g,
           od_w_kv_b, od_q_norm_g, od_k_norm_g, od_w_out):
    B, S, D = x.shape
    CL = ctx.shape[1]
    depth = mod_w.shape[0]
    assert depth == 2, "layer schedule below is written for one even and one odd layer"
    n_x = B * S
    n_all = n_x + B * CL
    x_rows = x.reshape(n_x, D)
    c_rows = ctx.reshape(B * CL, D)

    cc = jnp.concatenate([c, c_ctx[None, :], jnp.zeros((MOD_ROWS - B - 1, D), F32)], axis=0)
    mods = Mods(adaln_all(cc, mod_w, mod_b), B, S)

    def ffn(t_in, layer, m_rows):
        h = norm_mod([t_in], norm2_g[layer], mods, layer, SH2, SC2, tm=1024)
        act, w_down = matmul_swiglu(h, ffn_w_gu, ffn_w_down, layer, m_rows)
        return matmul_residual([act], w_down, 0, [t_in], mods, layer, G2, m_rows, tm=512, tn=512,
                               name="ffn_down")

    pool_width = ev_pool_scale.shape[1]
    h = norm_mod([x_rows, c_rows], norm1_g[0], mods, 0, SH1, SC1)
    w_in_ev = ev_w_in.astype(BF16)
    z = matmul(h, w_in_ev, 0, n_all, pool_width, F32, name="even_in")
    ya = pool_mixer(z, ev_pool_w[0], ev_pool_scale[0], B, S, CL)
    yb = sgu_mixer(h, w_in_ev, 0, ev_sgu_norm_g[0], ev_sgu_w_s[0], ev_sgu_b[0], pool_width)
    t = matmul_residual([ya, yb], ev_w_out.astype(BF16), 0, [x_rows, c_rows], mods, 0, G1, n_all,
                        name="even_out")
    t = ffn(t, 0, n_all)

    q_cols = MLA_Q_RANK + GQA_HEADS * GQA_HEAD_DIM
    cos_g, sin_g, cos_m, sin_m, sin_m_abs = _rope_tables(S)
    rope_block = _rope_block(S, n_x)
    h = norm_mod([t], norm1_g[1], mods, 1, SH1, SC1, tm=1024)
    w_in_bf = od_w_in[0].astype(BF16)
    cq, gq = q_in(h, w_in_bf[:, :q_cols], n_x, od_q_a_g[0], od_q_norm_g[0], cos_g, sin_g, rope_block)
    ckv, kr, gk, gv = kv_in(h, _odd_kv_weight(w_in_bf, q_cols), od_kv_a_g[0], od_k_norm_g[0], cos_g, sin_g,
                            cos_m, sin_m, rope_block)
    q_m = matmul_qrope(cq, _mla_q_weight(od_w_q_b[0]), cos_m, sin_m_abs, rope_block)
    kv_m = matmul(ckv, od_w_kv_b, 0, n_all, od_w_kv_b.shape[2], BF16, tm=512, tn=od_w_kv_b.shape[2],
                  name="mla_kv_up")
    o_m = mla_attention(q_m, kv_m, kr, B, S, CL)
    o_g = gqa_attention(gq, gk, gv, B, S, CL)
    t = matmul_residual([o_m, o_g], od_w_out.astype(BF16), 0, [t], mods, 1, G1, n_x, name="odd_out")
    t = ffn(t, 1, n_x)

    return final_norm(t, final_g, n_x).reshape(B, S, D)
```
